```python
import math
import jax, jax.numpy as jnp
from jax import lax
import numpy as np

D_MODEL = 1024
BATCH = 2
SEQ = 8192
DEPTH = 4

N_MIXERS = 2
CONV_CHANNELS = D_MODEL
CONV_WIDTH = 31
HEAD_DIM = 64
HEADS_PER_GROUP = 4
DILATION_PAIRS = ((128, 1), (512, 4), (2048, 16))
N_GROUPS = len(DILATION_PAIRS)
N_HEADS = N_GROUPS * HEADS_PER_GROUP
D_ATTN = N_HEADS * HEAD_DIM
N_BUCKETS = 32
REL_MAX_DISTANCE = 2048
D_FF = -(-8 * D_MODEL // (3 * 256)) * 256
EPS = 1e-6
N_CONV_LAYERS = (DEPTH + 1) // 2
N_ATTN_LAYERS = DEPTH // 2
NEG_INF = -1e30

kernel_name = "hybrid_conv_dilated_attn_trunk"


def rmsnorm(x, g):
    x32 = x.astype(jnp.float32)
    y = x32 * lax.rsqrt(jnp.mean(x32 * x32, axis=-1, keepdims=True) + EPS)
    return (y * g.astype(jnp.float32)).astype(x.dtype)


def layernorm(x, g, b):
    x32 = x.astype(jnp.float32)
    mu = jnp.mean(x32, axis=-1, keepdims=True)
    xc = x32 - mu
    var = jnp.mean(xc * xc, axis=-1, keepdims=True)
    y = xc * lax.rsqrt(var + EPS) * g.astype(jnp.float32) + b.astype(jnp.float32)
    return y.astype(x.dtype)


def t5_bucket(dist):
    max_exact = N_BUCKETS // 2
    n = jnp.maximum(dist, 0)
    nf = jnp.maximum(n, 1).astype(jnp.float32)
    large = max_exact + (jnp.log(nf / max_exact) / math.log(REL_MAX_DISTANCE / max_exact)
                         * (N_BUCKETS - max_exact)).astype(jnp.int32)
    large = jnp.minimum(large, N_BUCKETS - 1)
    return jnp.where(n < max_exact, n, large)


def conformer_conv(h, w_pw1, b_pw1, w_dw, b_dw, ln_g, ln_b, w_pw2, b_pw2):
    u = h @ w_pw1 + b_pw1
    a, gate = jnp.split(u, 2, axis=-1)
    u = a * jax.nn.sigmoid(gate)
    u = lax.conv_general_dilated(
        u, w_dw[:, None, :], window_strides=(1,), padding=[(CONV_WIDTH - 1, 0)],
        dimension_numbers=("NWC", "WIO", "NWC"), feature_group_count=CONV_CHANNELS) + b_dw
    u = jax.nn.silu(layernorm(u, ln_g, ln_b))
    return u @ w_pw2 + b_pw2


def dilated_group(q, k, v, bias_table, window, dilation):
    b_, s, h, hd = q.shape
    n_back = window // dilation
    seg = n_back * dilation
    s_pad = -(-s // seg) * seg
    nb = s_pad // seg
    pad = ((0, 0), (0, s_pad - s), (0, 0), (0, 0))

    def blocks(t):
        return jnp.pad(t, pad).reshape(b_, nb, n_back, dilation, h, hd)

    def with_prev(t):
        prev = jnp.pad(t[:, :-1], ((0, 0), (1, 0), (0, 0), (0, 0), (0, 0), (0, 0)))
        return jnp.concatenate([prev, t], axis=2)

    qb = blocks(q)
    kk = with_prev(blocks(k))
    vv = with_prev(blocks(v))

    i_idx = jnp.arange(n_back)[:, None]
    j_idx = jnp.arange(2 * n_back)[None, :]
    dist = i_idx + n_back - j_idx
    bias = jnp.transpose(bias_table[t5_bucket(dist * dilation)], (2, 0, 1)).astype(jnp.float32)
    valid = (dist >= 0) & (dist <= n_back)
    not_before_start = (jnp.arange(nb)[:, None, None] > 0) | (j_idx[None] >= n_back)
    mask = valid[None] & not_before_start

    logits = jnp.einsum("bnidhc,bnjdhc->bndhij", qb, kk) * (HEAD_DIM ** -0.5) + bias
    logits = jnp.where(mask[None, :, None, None], logits, NEG_INF)
    m = jnp.max(logits, axis=-1, keepdims=True)
    p = jnp.exp(logits - m)
    den = jnp.sum(p, axis=-1)
    o = jnp.einsum("bndhij,bnjdhc->bnidhc", p, vv)
    den_t = jnp.transpose(den, (0, 1, 4, 2, 3))
    lse_t = jnp.transpose(m[..., 0] + jnp.log(den), (0, 1, 4, 2, 3))
    o = (o / den_t[..., None]).reshape(b_, s_pad, h, hd)[:, :s]
    lse = lse_t.reshape(b_, s_pad, h)[:, :s]
    return o, lse


def dilated_attention(h, w_qkv, w_o, rel_bias):
    b_, s, _ = h.shape
    qkv = (h @ w_qkv).astype(jnp.float32).reshape(b_, s, 3, N_HEADS, HEAD_DIM)
    outs, lses = [], []
    for g, (window, dilation) in enumerate(DILATION_PAIRS):
        hs = slice(g * HEADS_PER_GROUP, (g + 1) * HEADS_PER_GROUP)
        o, l = dilated_group(qkv[:, :, 0, hs], qkv[:, :, 1, hs], qkv[:, :, 2, hs],
                             rel_bias[:, hs], window, dilation)
        outs.append(o)
        lses.append(l)
    alpha = jax.nn.softmax(jnp.stack(lses, axis=0), axis=0)
    o = jnp.concatenate([outs[g] * alpha[g][..., None] for g in range(N_GROUPS)], axis=2)
    return o.reshape(b_, s, D_ATTN).astype(h.dtype) @ w_o


def swiglu(h, w_gate, w_up, w_down):
    return (jax.nn.silu(h @ w_gate) * (h @ w_up)) @ w_down


def setup_inputs(seed: int = 0) -> dict:
    key = jax.random.key(seed)
    ks = jax.random.split(key, 20)
    f32 = jnp.float32

    def nrm(k, shape, scale):
        return jax.random.normal(k, shape, f32) * scale

    return {
        "x": nrm(ks[0], (BATCH, SEQ, D_MODEL), 1.0),
        "norm_mix": 1.0 + nrm(ks[1], (DEPTH, D_MODEL), 0.05),
        "norm_ffn": 1.0 + nrm(ks[2], (DEPTH, D_MODEL), 0.05),
        "final_norm": 1.0 + nrm(ks[3], (D_MODEL,), 0.05),
        "conv_w_pw1": nrm(ks[4], (N_CONV_LAYERS, D_MODEL, 2 * CONV_CHANNELS), D_MODEL ** -0.5),
        "conv_b_pw1": nrm(ks[5], (N_CONV_LAYERS, 2 * CONV_CHANNELS), 0.01),
        "conv_w_dw": nrm(ks[6], (N_CONV_LAYERS, CONV_WIDTH, CONV_CHANNELS), CONV_WIDTH ** -0.5),
        "conv_b_dw": nrm(ks[7], (N_CONV_LAYERS, CONV_CHANNELS), 0.01),
        "conv_ln_g": 1.0 + nrm(ks[8], (N_CONV_LAYERS, CONV_CHANNELS), 0.05),
        "conv_ln_b": nrm(ks[9], (N_CONV_LAYERS, CONV_CHANNELS), 0.01),
        "conv_w_pw2": nrm(ks[10], (N_CONV_LAYERS, CONV_CHANNELS, D_MODEL), CONV_CHANNELS ** -0.5),
        "conv_b_pw2": nrm(ks[11], (N_CONV_LAYERS, D_MODEL), 0.01),
        "attn_w_qkv": nrm(ks[12], (N_ATTN_LAYERS, D_MODEL, 3 * D_ATTN), D_MODEL ** -0.5),
        "attn_w_o": nrm(ks[13], (N_ATTN_LAYERS, D_ATTN, D_MODEL), D_ATTN ** -0.5),
        "rel_bias": nrm(ks[14], (N_BUCKETS, N_HEADS), 0.5),
        "ffn_w_gate": nrm(ks[15], (DEPTH, D_MODEL, D_FF), D_MODEL ** -0.5),
        "ffn_w_up": nrm(ks[16], (DEPTH, D_MODEL, D_FF), D_MODEL ** -0.5),
        "ffn_w_down": nrm(ks[17], (DEPTH, D_FF, D_MODEL), D_FF ** -0.5),
    }


def reference(x, norm_mix, norm_ffn, final_norm, conv_w_pw1, conv_b_pw1, conv_w_dw, conv_b_dw,
              conv_ln_g, conv_ln_b, conv_w_pw2, conv_b_pw2, attn_w_qkv, attn_w_o, rel_bias,
              ffn_w_gate, ffn_w_up, ffn_w_down):
    for i in range(DEPTH):
        h = rmsnorm(x, norm_mix[i])
        j = i // N_MIXERS
        if i % N_MIXERS == 0:
            x = x + conformer_conv(h, conv_w_pw1[j], conv_b_pw1[j], conv_w_dw[j], conv_b_dw[j],
                                   conv_ln_g[j], conv_ln_b[j], conv_w_pw2[j], conv_b_pw2[j])
        else:
            x = x + dilated_attention(h, attn_w_qkv[j], attn_w_o[j], rel_bias)
        h = rmsnorm(x, norm_ffn[i])
        x = x + swiglu(h, ffn_w_gate[i], ffn_w_up[i], ffn_w_down[i])
    return rmsnorm(x, final_norm)
```

```python
import functools
import math

import numpy as np
import jax
import jax.numpy as jnp
from jax import lax
from jax.experimental import pallas as pl
from jax.experimental.pallas import tpu as pltpu

EPS = 1e-6
NEG_INF = -1e30
HEAD_DIM = 64
HEADS_PER_GROUP = 4
GROUP_WIDTH = HEADS_PER_GROUP * HEAD_DIM
DILATION_PAIRS = ((128, 1), (512, 4), (2048, 16))
N_GROUPS = len(DILATION_PAIRS)
N_BUCKETS = 32
REL_MAX_DISTANCE = 2048
CONV_WIDTH = 31
CONV_HALO = 32

VMEM_LIMIT_BYTES = 56 * 1024 * 1024
TOKEN_TILE = 512
ATTN_Q_BLOCKS = 4


def _const_spec(shape):
    nd = len(shape)
    return pl.BlockSpec(shape, lambda *_: (0,) * nd, pipeline_mode=pl.Buffered(1))


def _rmsnorm_f32(x, g):
    return x * lax.rsqrt(jnp.mean(x * x, axis=-1, keepdims=True) + EPS) * g


def _mm(a, b):
    return jnp.dot(a, b, preferred_element_type=jnp.float32)


def _ffn_body(x_ref, g_ref, wg_ref, wu_ref, wd_ref, fg_ref, o_ref, h_ref, *, ff_chunk, final_norm):
    x = x_ref[...]
    xn = _rmsnorm_f32(x, g_ref[...]).astype(jnp.bfloat16)
    d_ff = wg_ref.shape[1]
    for c in range(d_ff // ff_chunk):
        cs = slice(c * ff_chunk, (c + 1) * ff_chunk)
        gate = _mm(xn, wg_ref[:, cs])
        up = _mm(xn, wu_ref[:, cs])
        h_ref[:, cs] = (gate * jax.nn.sigmoid(gate) * up).astype(jnp.bfloat16)
    y = x + _mm(h_ref[...], wd_ref[...])
    if final_norm:
        y = _rmsnorm_f32(y, fg_ref[...])
    o_ref[...] = y


def _ffn(x2d, g, wg, wu, wd, fg, final_norm):
    n, d = x2d.shape
    d_ff = wg.shape[1]
    tm = TOKEN_TILE
    ff_chunk = 256
    assert n % tm == 0 and d_ff % ff_chunk == 0
    return pl.pallas_call(
        functools.partial(_ffn_body, ff_chunk=ff_chunk, final_norm=final_norm),
        grid=(n // tm,),
        in_specs=[
            pl.BlockSpec((tm, d), lambda i: (i, 0)),
            _const_spec((1, d)),
            _const_spec((d, d_ff)),
            _const_spec((d, d_ff)),
            _const_spec((d_ff, d)),
            _const_spec((1, d)),
        ],
        out_specs=pl.BlockSpec((tm, d), lambda i: (i, 0)),
        out_shape=jax.ShapeDtypeStruct((n, d), jnp.float32),
        scratch_shapes=[pltpu.VMEM((tm, d_ff), jnp.bfloat16)],
        compiler_params=pltpu.CompilerParams(
            dimension_semantics=("arbitrary",), vmem_limit_bytes=VMEM_LIMIT_BYTES),
        name="ffn",
    )(x2d, g, wg, wu, wd, fg)


def _conv_body(x_ref, g_ref, w1_ref, b1_ref, wdw_ref, bdw_ref, lng_ref, lnb_ref, w2_ref, b2_ref,
               o_ref, u_ref, c_ref, *, row_chunk):
    ts, d = x_ref.shape[1], x_ref.shape[2]
    c = w2_ref.shape[0]
    n_slab = c // 128

    @pl.when(pl.program_id(1) == 0)
    def _():
        u_ref[:, 0:CONV_HALO, :] = jnp.zeros((n_slab, CONV_HALO, 128), jnp.float32)

    x = x_ref[0]
    xn = _rmsnorm_f32(x, g_ref[...]).astype(jnp.bfloat16)
    for j in range(n_slab):
        a = _mm(xn, w1_ref[:, j * 128:(j + 1) * 128]) + b1_ref[:, j * 128:(j + 1) * 128]
        gate = _mm(xn, w1_ref[:, c + j * 128:c + (j + 1) * 128]) + b1_ref[:, c + j * 128:c + (j + 1) * 128]
        u_ref[j, CONV_HALO:CONV_HALO + ts, :] = a * jax.nn.sigmoid(gate)

    base = CONV_HALO - (CONV_WIDTH - 1)
    for j in range(n_slab):
        ls = slice(j * 128, (j + 1) * 128)

        def chunk(i, carry, j=j, ls=ls):
            r0 = pl.multiple_of(i * row_chunk, row_chunk)
            acc = jnp.broadcast_to(bdw_ref[:, ls], (row_chunk, 128))
            for k in range(CONV_WIDTH):
                acc = acc + wdw_ref[k:k + 1, ls] * u_ref[j, pl.ds(r0 + base + k, row_chunk), :]
            c_ref[pl.ds(r0, row_chunk), ls] = acc
            return carry

        lax.fori_loop(0, ts // row_chunk, chunk, 0)

    for j in range(n_slab):
        u_ref[j, 0:CONV_HALO, :] = u_ref[j, ts:ts + CONV_HALO, :]

    v = c_ref[...]
    mu = jnp.mean(v, axis=-1, keepdims=True)
    vc = v - mu
    var = jnp.mean(vc * vc, axis=-1, keepdims=True)
    y = vc * lax.rsqrt(var + EPS) * lng_ref[...] + lnb_ref[...]
    y = (y * jax.nn.sigmoid(y)).astype(jnp.bfloat16)
    o_ref[0] = x + _mm(y, w2_ref[...]) + b2_ref[...]


def _conv_module(x, g, w1, b1, wdw, bdw, lng, lnb, w2, b2):
    b, s, d = x.shape
    c = w2.shape[0]
    ts = TOKEN_TILE
    assert s % ts == 0 and c % 128 == 0
    return pl.pallas_call(
        functools.partial(_conv_body, row_chunk=64),
        grid=(b, s // ts),
        in_specs=[
            pl.BlockSpec((1, ts, d), lambda bi, si: (bi, si, 0)),
            _const_spec((1, d)),
            _const_spec((d, 2 * c)),
            _const_spec((1, 2 * c)),
            _const_spec((CONV_WIDTH, c)),
            _const_spec((1, c)),
            _const_spec((1, c)),
            _const_spec((1, c)),
            _const_spec((c, d)),
            _const_spec((1, d)),
        ],
        out_specs=pl.BlockSpec((1, ts, d), lambda bi, si: (bi, si, 0)),
        out_shape=jax.ShapeDtypeStruct((b, s, d), jnp.float32),
        scratch_shapes=[pltpu.VMEM((c // 128, ts + CONV_HALO, 128), jnp.float32),
                        pltpu.VMEM((ts, c), jnp.float32)],
        compiler_params=pltpu.CompilerParams(
            dimension_semantics=("arbitrary", "arbitrary"), vmem_limit_bytes=VMEM_LIMIT_BYTES),
        name="conv_module",
    )(x, g, w1, b1, wdw, bdw, lng, lnb, w2, b2)


def _t5_bucket_np(dist):
    max_exact = N_BUCKETS // 2
    n = np.maximum(dist, 0)
    nf = np.maximum(n, 1).astype(np.float32)
    large = max_exact + (np.log(nf / np.float32(max_exact)) / np.float32(math.log(REL_MAX_DISTANCE / max_exact))
                         * np.float32(N_BUCKETS - max_exact)).astype(np.int32)
    large = np.minimum(large, N_BUCKETS - 1)
    return np.where(n < max_exact, n, large).astype(np.int32)


def _bucket_index_tables(n_back):
    i_idx = np.arange(n_back)[:, None]
    j_idx = np.arange(2 * n_back)[None, :]
    dist = i_idx + n_back - j_idx
    valid = (dist >= 0) & (dist <= n_back)
    tabs = [np.where(valid, _t5_bucket_np(dist * dil), -1) for _, dil in DILATION_PAIRS]
    return np.stack(tabs).astype(np.int32)


def _bias_body(tab_ref, idx_ref, o_ref):
    g = pl.program_id(0)
    idx = idx_ref[0]
    for h in range(HEADS_PER_GROUP):
        acc = jnp.full(idx.shape, NEG_INF, jnp.float32)
        for bkt in range(N_BUCKETS):
            acc = jnp.where(idx == bkt, tab_ref[bkt, g * HEADS_PER_GROUP + h], acc)
        o_ref[0, h] = acc


def _bias_tables(rel_bias, n_back):
    idx = jnp.asarray(_bucket_index_tables(n_back))
    return pl.pallas_call(
        _bias_body,
        grid=(N_GROUPS,),
        in_specs=[pl.BlockSpec(memory_space=pltpu.SMEM),
                  pl.BlockSpec((1, n_back, 2 * n_back), lambda g: (g, 0, 0))],
        out_specs=pl.BlockSpec((1, HEADS_PER_GROUP, n_back, 2 * n_back), lambda g: (g, 0, 0, 0)),
        out_shape=jax.ShapeDtypeStruct((N_GROUPS, HEADS_PER_GROUP, n_back, 2 * n_back), jnp.float32),
        name="rel_bias_tables",
    )(rel_bias, idx)


def _qkv_body(x_ref, g_ref, w_ref, o_ref, *, d_attn):
    xn = _rmsnorm_f32(x_ref[...], g_ref[...]).astype(jnp.bfloat16)
    o_ref[:, :d_attn] = (_mm(xn, w_ref[:, :d_attn]) * (HEAD_DIM ** -0.5)).astype(jnp.bfloat16)
    o_ref[:, d_attn:] = _mm(xn, w_ref[:, d_attn:]).astype(jnp.bfloat16)


def _qkv_proj(x2d, g, w):
    n, d = x2d.shape
    dq = w.shape[1]
    tm = TOKEN_TILE
    return pl.pallas_call(
        functools.partial(_qkv_body, d_attn=dq // 3),
        grid=(n // tm,),
        in_specs=[pl.BlockSpec((tm, d), lambda i: (i, 0)), _const_spec((1, d)), _const_spec((d, dq))],
        out_specs=pl.BlockSpec((tm, dq), lambda i: (i, 0)),
        out_shape=jax.ShapeDtypeStruct((n, dq), jnp.bfloat16),
        compiler_params=pltpu.CompilerParams(
            dimension_semantics=("arbitrary",), vmem_limit_bytes=VMEM_LIMIT_BYTES),
        name="qkv_proj",
    )(x2d, g, w)


def _attn_body(q_ref, kp_ref, kc_ref, vp_ref, vc_ref, bias_ref, o_ref, l_ref, *, n_back):
    first = pl.program_id(2) == 0
    tq = q_ref.shape[1]
    for h in range(HEADS_PER_GROUP):
        hs = slice(h * HEAD_DIM, (h + 1) * HEAD_DIM)
        bias = bias_ref[0, h]
        for i in range(tq // n_back):
            rows = slice(i * n_back, (i + 1) * n_back)
            q = q_ref[0, rows, hs]
            if i == 0:
                k = jnp.concatenate([kp_ref[0, :, hs], kc_ref[0, 0:n_back, hs]], axis=0)
                v = jnp.concatenate([vp_ref[0, :, hs], vc_ref[0, 0:n_back, hs]], axis=0)
            else:
                k = kc_ref[0, (i - 1) * n_back:(i + 1) * n_back, hs]
                v = vc_ref[0, (i - 1) * n_back:(i + 1) * n_back, hs]
            logits = lax.dot_general(q, k, (((1,), (1,)), ((), ())),
                                     preferred_element_type=jnp.float32) + bias
            if i == 0:
                col = lax.broadcasted_iota(jnp.int32, logits.shape, 1)
                logits = jnp.where(jnp.logical_and(first, col < n_back), NEG_INF, logits)
            m = jnp.max(logits, axis=-1, keepdims=True)
            p = jnp.exp(logits - m)
            den = jnp.sum(p, axis=-1, keepdims=True)
            o = _mm(p.astype(jnp.bfloat16), v) / den
            o_ref[0, rows, hs] = o
            l_ref[0, rows, hs] = jnp.broadcast_to(m + jnp.log(den), (n_back, HEAD_DIM))


def _attn_group(qkv, bias, g, dilation, n_back):
    b, s, dq = qkv.shape
    sub = s // dilation
    tq = min(ATTN_Q_BLOCKS * n_back, sub)
    assert sub % tq == 0 and dq % GROUP_WIDTH == 0
    cols = dq // GROUP_WIDTH
    kcol, vcol = cols // 3, 2 * cols // 3
    view = qkv.reshape(b, sub, dilation * dq)
    nprev = tq // n_back

    def cur(off):
        return pl.BlockSpec((1, tq, GROUP_WIDTH), lambda bi, r, n: (bi, n, r * cols + off))

    def prev(off):
        return pl.BlockSpec((1, n_back, GROUP_WIDTH),
                            lambda bi, r, n: (bi, jnp.maximum(n * nprev - 1, 0), r * cols + off))

    out_spec = pl.BlockSpec((1, tq, GROUP_WIDTH), lambda bi, r, n: (bi, n, r))
    out_sds = jax.ShapeDtypeStruct((b, sub, dilation * GROUP_WIDTH), jnp.float32)
    o, l = pl.pallas_call(
        functools.partial(_attn_body, n_back=n_back),
        grid=(b, dilation, sub // tq),
        in_specs=[cur(g), prev(kcol + g), cur(kcol + g), prev(vcol + g), cur(vcol + g),
                  pl.BlockSpec((1, HEADS_PER_GROUP, n_back, 2 * n_back), lambda bi, r, n: (g, 0, 0, 0))],
        out_specs=[out_spec, out_spec],
        out_shape=[out_sds, out_sds],
        compiler_params=pltpu.CompilerParams(
            dimension_semantics=("arbitrary", "arbitrary", "arbitrary"), vmem_limit_bytes=VMEM_LIMIT_BYTES),
        name=f"dilated_attn_g{g}",
    )(view, view, view, view, view, bias)
    return o.reshape(b * s, GROUP_WIDTH), l.reshape(b * s, GROUP_WIDTH)


def _attn_out_body(x_ref, o0_ref, o1_ref, o2_ref, l0_ref, l1_ref, l2_ref, w_ref, y_ref):
    ls = [l0_ref[...], l1_ref[...], l2_ref[...]]
    mx = jnp.maximum(jnp.maximum(ls[0], ls[1]), ls[2])
    es = [jnp.exp(l - mx) for l in ls]
    tot = es[0] + es[1] + es[2]
    y = x_ref[...]
    for gi, o_ref in enumerate((o0_ref, o1_ref, o2_ref)):
        og = (o_ref[...] * (es[gi] / tot)).astype(jnp.bfloat16)
        y = y + _mm(og, w_ref[gi * GROUP_WIDTH:(gi + 1) * GROUP_WIDTH, :])
    y_ref[...] = y


def _attn_out_proj(x2d, outs, lses, w_o):
    n, d = x2d.shape
    tm = TOKEN_TILE
    tok = lambda w: pl.BlockSpec((tm, w), lambda i: (i, 0))
    return pl.pallas_call(
        _attn_out_body,
        grid=(n // tm,),
        in_specs=[tok(d)] + [tok(GROUP_WIDTH)] * 6 + [_const_spec(w_o.shape)],
        out_specs=tok(d),
        out_shape=jax.ShapeDtypeStruct((n, d), jnp.float32),
        compiler_params=pltpu.CompilerParams(
            dimension_semantics=("arbitrary",), vmem_limit_bytes=VMEM_LIMIT_BYTES),
        name="attn_out_proj",
    )(x2d, *outs, *lses, w_o)


def _dilated_attention(x, g, w_qkv, w_o, bias):
    b, s, d = x.shape
    x2d = x.reshape(b * s, d)
    qkv = _qkv_proj(x2d, g, w_qkv).reshape(b, s, w_qkv.shape[1])
    outs, lses = [], []
    for gi, (window, dilation) in enumerate(DILATION_PAIRS):
        o, l = _attn_group(qkv, bias, gi, dilation, window // dilation)
        outs.append(o)
        lses.append(l)
    return _attn_out_proj(x2d, outs, lses, w_o).reshape(b, s, d)


def kernel(x, norm_mix, norm_ffn, final_norm, conv_w_pw1, conv_b_pw1, conv_w_dw, conv_b_dw, conv_ln_g, conv_ln_b, conv_w_pw2, conv_b_pw2, attn_w_qkv, attn_w_o, rel_bias, ffn_w_gate, ffn_w_up, ffn_w_down):
    b, s, d = x.shape
    depth = norm_mix.shape[0]
    bf = jnp.bfloat16
    n_back = DILATION_PAIRS[0][0] // DILATION_PAIRS[0][1]
    assert all(w // dl == n_back and s % w == 0 for w, dl in DILATION_PAIRS)
    bias = _bias_tables(rel_bias, n_back)
    row = lambda v: v.reshape(1, -1)
    for i in range(depth):
        j = i // 2
        if i % 2 == 0:
            x = _conv_module(x, row(norm_mix[i]), conv_w_pw1[j].astype(bf), row(conv_b_pw1[j]), conv_w_dw[j],
                             row(conv_b_dw[j]), row(conv_ln_g[j]), row(conv_ln_b[j]),
                             conv_w_pw2[j].astype(bf), row(conv_b_pw2[j]))
        else:
            x = _dilated_attention(x, row(norm_mix[i]), attn_w_qkv[j].astype(bf), attn_w_o[j].astype(bf), bias)
        x = _ffn(x.reshape(b * s, d), row(norm_ffn[i]), ffn_w_gate[i].astype(bf), ffn_w_up[i].astype(bf),
                 ffn_w_down[i].astype(bf), row(final_norm), final_norm=(i == depth - 1)).reshape(b, s, d)
    return x
```

```python
import functools
import math

import numpy as np
import jax
import jax.numpy as jnp
from jax import lax
from jax.experimental import pallas as pl
from jax.experimental.pallas import tpu as pltpu

EPS = 1e-6
NEG_INF = -1e30
HEAD_DIM = 64
HEADS_PER_GROUP = 4
GROUP_WIDTH = HEADS_PER_GROUP * HEAD_DIM
DILATION_PAIRS = ((128, 1), (512, 4), (2048, 16))
N_GROUPS = len(DILATION_PAIRS)
N_BUCKETS = 32
REL_MAX_DISTANCE = 2048
CONV_WIDTH = 31
CONV_HALO = 32

VMEM_LIMIT_BYTES = 56 * 1024 * 1024
TOKEN_TILE = 512
ATTN_Q_BLOCKS = 4


def _const_spec(shape):
    nd = len(shape)
    return pl.BlockSpec(shape, lambda *_: (0,) * nd, pipeline_mode=pl.Buffered(1))


def _layer_spec(stack_shape, layer):
    return pl.BlockSpec((None,) + tuple(stack_shape[1:]), lambda *_: (layer, 0, 0), pipeline_mode=pl.Buffered(1))


def _cast_body(x_ref, o_ref):
    o_ref[...] = x_ref[...].astype(o_ref.dtype)


def _to_bf16(w):
    l, r, c = w.shape
    rows = 512
    assert (l * r) % rows == 0
    out = pl.pallas_call(
        _cast_body,
        grid=(l * r // rows,),
        in_specs=[pl.BlockSpec((rows, c), lambda i: (i, 0))],
        out_specs=pl.BlockSpec((rows, c), lambda i: (i, 0)),
        out_shape=jax.ShapeDtypeStruct((l * r, c), jnp.bfloat16),
        compiler_params=pltpu.CompilerParams(
            dimension_semantics=("arbitrary",), vmem_limit_bytes=VMEM_LIMIT_BYTES),
        name="to_bf16",
    )(w.reshape(l * r, c))
    return out.reshape(l, r, c)


def _rmsnorm_f32(x, g):
    return x * lax.rsqrt(jnp.mean(x * x, axis=-1, keepdims=True) + EPS) * g


def _mm(a, b):
    return jnp.dot(a, b, preferred_element_type=jnp.float32)


def _ffn_body(x_ref, g_ref, wg_ref, wu_ref, wd_ref, fg_ref, o_ref, h_ref, *, ff_chunk, final_norm):
    x = x_ref[...]
    xn = _rmsnorm_f32(x, g_ref[...]).astype(jnp.bfloat16)
    d_ff = wg_ref.shape[1]
    for c in range(d_ff // ff_chunk):
        cs = slice(c * ff_chunk, (c + 1) * ff_chunk)
        gate = _mm(xn, wg_ref[:, cs])
        up = _mm(xn, wu_ref[:, cs])
        h_ref[:, cs] = (gate * jax.nn.sigmoid(gate) * up).astype(jnp.bfloat16)
    y = x + _mm(h_ref[...], wd_ref[...])
    if final_norm:
        y = _rmsnorm_f32(y, fg_ref[...])
    o_ref[...] = y


def _ffn(x2d, layer, g, wg, wu, wd, fg, final_norm):
    n, d = x2d.shape
    d_ff = wg.shape[2]
    tm = TOKEN_TILE
    ff_chunk = 256
    assert n % tm == 0 and d_ff % ff_chunk == 0
    return pl.pallas_call(
        functools.partial(_ffn_body, ff_chunk=ff_chunk, final_norm=final_norm),
        grid=(n // tm,),
        in_specs=[pl.BlockSpec((tm, d), lambda i: (i, 0))]
        + [_layer_spec(a.shape, layer) for a in (g, wg, wu, wd)] + [_const_spec(fg.shape)],
        out_specs=pl.BlockSpec((tm, d), lambda i: (i, 0)),
        out_shape=jax.ShapeDtypeStruct((n, d), jnp.float32),
        scratch_shapes=[pltpu.VMEM((tm, d_ff), jnp.bfloat16)],
        compiler_params=pltpu.CompilerParams(
            dimension_semantics=("arbitrary",), vmem_limit_bytes=VMEM_LIMIT_BYTES),
        name="ffn",
    )(x2d, g, wg, wu, wd, fg)


def _conv_body(x_ref, g_ref, w1_ref, b1_ref, wdw_ref, bdw_ref, lng_ref, lnb_ref, w2_ref, b2_ref,
               o_ref, u_ref, c_ref, *, row_chunk):
    ts, d = x_ref.shape[1], x_ref.shape[2]
    c = w2_ref.shape[0]
    n_slab = c // 128

    @pl.when(pl.program_id(1) == 0)
    def _():
        u_ref[:, 0:CONV_HALO, :] = jnp.zeros((n_slab, CONV_HALO, 128), jnp.float32)

    x = x_ref[0]
    xn = _rmsnorm_f32(x, g_ref[...]).astype(jnp.bfloat16)
    for j in range(n_slab):
        a = _mm(xn, w1_ref[:, j * 128:(j + 1) * 128]) + b1_ref[:, j * 128:(j + 1) * 128]
        gate = _mm(xn, w1_ref[:, c + j * 128:c + (j + 1) * 128]) + b1_ref[:, c + j * 128:c + (j + 1) * 128]
        u_ref[j, CONV_HALO:CONV_HALO + ts, :] = a * jax.nn.sigmoid(gate)

    base = CONV_HALO - (CONV_WIDTH - 1)
    for j in range(n_slab):
        ls = slice(j * 128, (j + 1) * 128)

        def chunk(i, carry, j=j, ls=ls):
            r0 = pl.multiple_of(i * row_chunk, row_chunk)
            acc = jnp.broadcast_to(bdw_ref[:, ls], (row_chunk, 128))
            for k in range(CONV_WIDTH):
                acc = acc + wdw_ref[k:k + 1, ls] * u_ref[j, pl.ds(r0 + base + k, row_chunk), :]
            c_ref[pl.ds(r0, row_chunk), ls] = acc
            return carry

        lax.fori_loop(0, ts // row_chunk, chunk, 0)

    for j in range(n_slab):
        u_ref[j, 0:CONV_HALO, :] = u_ref[j, ts:ts + CONV_HALO, :]

    v = c_ref[...]
    mu = jnp.mean(v, axis=-1, keepdims=True)
    vc = v - mu
    var = jnp.mean(vc * vc, axis=-1, keepdims=True)
    y = vc * lax.rsqrt(var + EPS) * lng_ref[...] + lnb_ref[...]
    y = (y * jax.nn.sigmoid(y)).astype(jnp.bfloat16)
    o_ref[0] = x + _mm(y, w2_ref[...]) + b2_ref[...]


def _conv_module(x, layer, mix_layer, g, w1, b1, wdw, bdw, lng, lnb, w2, b2):
    b, s, d = x.shape
    c = w2.shape[1]
    ts = TOKEN_TILE
    assert s % ts == 0 and c % 128 == 0
    return pl.pallas_call(
        functools.partial(_conv_body, row_chunk=64),
        grid=(b, s // ts),
        in_specs=[pl.BlockSpec((1, ts, d), lambda bi, si: (bi, si, 0)), _layer_spec(g.shape, mix_layer)]
        + [_layer_spec(a.shape, layer) for a in (w1, b1, wdw, bdw, lng, lnb, w2, b2)],
        out_specs=pl.BlockSpec((1, ts, d), lambda bi, si: (bi, si, 0)),
        out_shape=jax.ShapeDtypeStruct((b, s, d), jnp.float32),
        scratch_shapes=[pltpu.VMEM((c // 128, ts + CONV_HALO, 128), jnp.float32),
                        pltpu.VMEM((ts, c), jnp.float32)],
        compiler_params=pltpu.CompilerParams(
            dimension_semantics=("arbitrary", "arbitrary"), vmem_limit_bytes=VMEM_LIMIT_BYTES),
        name="conv_module",
    )(x, g, w1, b1, wdw, bdw, lng, lnb, w2, b2)


def _t5_bucket_np(dist):
    max_exact = N_BUCKETS // 2
    n = np.maximum(dist, 0)
    nf = np.maximum(n, 1).astype(np.float32)
    large = max_exact + (np.log(nf / np.float32(max_exact)) / np.float32(math.log(REL_MAX_DISTANCE / max_exact))
                         * np.float32(N_BUCKETS - max_exact)).astype(np.int32)
    large = np.minimum(large, N_BUCKETS - 1)
    return np.where(n < max_exact, n, large).astype(np.int32)


def _bucket_index_tables(n_back):
    i_idx = np.arange(n_back)[:, None]
    j_idx = np.arange(2 * n_back)[None, :]
    dist = i_idx + n_back - j_idx
    valid = (dist >= 0) & (dist <= n_back)
    tabs = [np.where(valid, _t5_bucket_np(dist * dil), -1) for _, dil in DILATION_PAIRS]
    return np.stack(tabs).astype(np.int32)


def _bias_body(tab_ref, idx_ref, o_ref):
    g = pl.program_id(0)
    idx = idx_ref[0]
    for h in range(HEADS_PER_GROUP):
        acc = jnp.full(idx.shape, NEG_INF, jnp.float32)
        for bkt in range(N_BUCKETS):
            acc = jnp.where(idx == bkt, tab_ref[bkt, g * HEADS_PER_GROUP + h], acc)
        o_ref[0, h] = acc


def _bias_tables(rel_bias, n_back):
    idx = jnp.asarray(_bucket_index_tables(n_back))
    return pl.pallas_call(
        _bias_body,
        grid=(N_GROUPS,),
        in_specs=[pl.BlockSpec(memory_space=pltpu.SMEM),
                  pl.BlockSpec((1, n_back, 2 * n_back), lambda g: (g, 0, 0))],
        out_specs=pl.BlockSpec((1, HEADS_PER_GROUP, n_back, 2 * n_back), lambda g: (g, 0, 0, 0)),
        out_shape=jax.ShapeDtypeStruct((N_GROUPS, HEADS_PER_GROUP, n_back, 2 * n_back), jnp.float32),
        name="rel_bias_tables",
    )(rel_bias, idx)


def _qkv_body(x_ref, g_ref, w_ref, *refs, d_attn):
    o_refs, scr_ref = refs[:N_GROUPS], refs[N_GROUPS]
    tm = x_ref.shape[0]
    xn = _rmsnorm_f32(x_ref[...], g_ref[...]).astype(jnp.bfloat16)
    for gi, (_, dil) in enumerate(DILATION_PAIRS):
        for part in range(3):
            c0 = part * d_attn + gi * GROUP_WIDTH
            y = _mm(xn, w_ref[:, c0:c0 + GROUP_WIDTH])
            if part == 0:
                y = y * (HEAD_DIM ** -0.5)
            if dil == 1:
                o_refs[gi][:, part * GROUP_WIDTH:(part + 1) * GROUP_WIDTH] = y.astype(jnp.bfloat16)
                continue
            for half in range(GROUP_WIDTH // 128):
                scr_ref[half] = y[:, half * 128:(half + 1) * 128]
            for r in range(dil):
                for half in range(GROUP_WIDTH // 128):
                    lane0 = (r * 3 + part) * GROUP_WIDTH + half * 128
                    o_refs[gi][:, lane0:lane0 + 128] = (
                        scr_ref[half, pl.ds(r, tm // dil, stride=dil), :].astype(jnp.bfloat16))


def _qkv_proj(x2d, layer, mix_layer, g, w):
    n, d = x2d.shape
    dq = w.shape[2]
    tm = TOKEN_TILE
    gw3 = 3 * GROUP_WIDTH
    return pl.pallas_call(
        functools.partial(_qkv_body, d_attn=dq // 3),
        grid=(n // tm,),
        in_specs=[pl.BlockSpec((tm, d), lambda i: (i, 0)), _layer_spec(g.shape, mix_layer),
                  _layer_spec(w.shape, layer)],
        out_specs=[pl.BlockSpec((tm // dil, dil * gw3), lambda i: (i, 0)) for _, dil in DILATION_PAIRS],
        out_shape=[jax.ShapeDtypeStruct((n // dil, dil * gw3), jnp.bfloat16) for _, dil in DILATION_PAIRS],
        scratch_shapes=[pltpu.VMEM((GROUP_WIDTH // 128, tm, 128), jnp.float32)],
        compiler_params=pltpu.CompilerParams(
            dimension_semantics=("arbitrary",), vmem_limit_bytes=VMEM_LIMIT_BYTES),
        name="qkv_proj",
    )(x2d, g, w)


def _attn_body(q_ref, kp_ref, kc_ref, vp_ref, vc_ref, bias_ref, o_ref, l_ref, *, n_back):
    first = pl.program_id(2) == 0
    tq = q_ref.shape[1]
    for h in range(HEADS_PER_GROUP):
        hs = slice(h * HEAD_DIM, (h + 1) * HEAD_DIM)
        bias = bias_ref[0, h]
        for i in range(tq // n_back):
            rows = slice(i * n_back, (i + 1) * n_back)
            q = q_ref[0, rows, hs]
            if i == 0:
                k = jnp.concatenate([kp_ref[0, :, hs], kc_ref[0, 0:n_back, hs]], axis=0)
                v = jnp.concatenate([vp_ref[0, :, hs], vc_ref[0, 0:n_back, hs]], axis=0)
            else:
                k = kc_ref[0, (i - 1) * n_back:(i + 1) * n_back, hs]
                v = vc_ref[0, (i - 1) * n_back:(i + 1) * n_back, hs]
            logits = lax.dot_general(q, k, (((1,), (1,)), ((), ())),
                                     preferred_element_type=jnp.float32) + bias
            if i == 0:
                col = lax.broadcasted_iota(jnp.int32, logits.shape, 1)
                logits = jnp.where(jnp.logical_and(first, col < n_back), NEG_INF, logits)
            m = jnp.max(logits, axis=-1, keepdims=True)
            p = jnp.exp(logits - m)
            den = jnp.sum(p, axis=-1, keepdims=True)
            o = _mm(p.astype(jnp.bfloat16), v) / den
            o_ref[0, rows, hs] = o
            l_ref[0, rows, hs] = jnp.broadcast_to(m + jnp.log(den), (n_back, HEAD_DIM))


def _attn_group(view, bias, g, dilation, n_back):
    b, sub, _ = view.shape
    tq = min(ATTN_Q_BLOCKS * n_back, sub)
    assert sub % tq == 0
    nprev = tq // n_back
    q_col, k_col, v_col = 0, 1, 2

    def cur(off):
        return pl.BlockSpec((1, tq, GROUP_WIDTH), lambda bi, r, n: (bi, n, r * 3 + off))

    def prev(off):
        return pl.BlockSpec((1, n_back, GROUP_WIDTH),
                            lambda bi, r, n: (bi, jnp.maximum(n * nprev - 1, 0), r * 3 + off))

    out_spec = pl.BlockSpec((1, tq, GROUP_WIDTH), lambda bi, r, n: (bi, n, r))
    out_sds = jax.ShapeDtypeStruct((b, sub, dilation * GROUP_WIDTH), jnp.float32)
    return pl.pallas_call(
        functools.partial(_attn_body, n_back=n_back),
        grid=(b, dilation, sub // tq),
        in_specs=[cur(q_col), prev(k_col), cur(k_col), prev(v_col), cur(v_col),
                  pl.BlockSpec((1, HEADS_PER_GROUP, n_back, 2 * n_back), lambda bi, r, n: (g, 0, 0, 0))],
        out_specs=[out_spec, out_spec],
        out_shape=[out_sds, out_sds],
        compiler_params=pltpu.CompilerParams(
            dimension_semantics=("arbitrary", "arbitrary", "arbitrary"), vmem_limit_bytes=VMEM_LIMIT_BYTES),
        name=f"dilated_attn_g{g}",
    )(view, view, view, view, view, bias)


def _attn_out_body(x_ref, o0_ref, o1_ref, o2_ref, l0_ref, l1_ref, l2_ref, w_ref, y_ref, scr_ref):
    tm = x_ref.shape[0]
    halves = GROUP_WIDTH // 128

    def token_major(ref, gi, slot):
        dil = DILATION_PAIRS[gi][1]
        if dil == 1:
            return ref[...]
        for r in range(dil):
            for half in range(halves):
                lane0 = r * GROUP_WIDTH + half * 128
                scr_ref[slot, half, pl.ds(r, tm // dil, stride=dil), :] = ref[:, lane0:lane0 + 128]
        return jnp.concatenate([scr_ref[slot, half] for half in range(halves)], axis=1)

    o_refs, l_refs = (o0_ref, o1_ref, o2_ref), (l0_ref, l1_ref, l2_ref)
    ls = [token_major(l_refs[gi], gi, gi) for gi in range(N_GROUPS)]
    mx = jnp.maximum(jnp.maximum(ls[0], ls[1]), ls[2])
    es = [jnp.exp(l - mx) for l in ls]
    tot = es[0] + es[1] + es[2]
    y = x_ref[...]
    for gi in range(N_GROUPS):
        og = (token_major(o_refs[gi], gi, N_GROUPS + gi) * (es[gi] / tot)).astype(jnp.bfloat16)
        y = y + _mm(og, w_ref[gi * GROUP_WIDTH:(gi + 1) * GROUP_WIDTH, :])
    y_ref[...] = y


def _attn_out_proj(x2d, outs, lses, layer, w_o):
    n, d = x2d.shape
    tm = TOKEN_TILE
    tok = lambda w: pl.BlockSpec((tm, w), lambda i: (i, 0))
    grp = [pl.BlockSpec((tm // dil, dil * GROUP_WIDTH), lambda i: (i, 0)) for _, dil in DILATION_PAIRS]
    return pl.pallas_call(
        _attn_out_body,
        grid=(n // tm,),
        in_specs=[tok(d)] + grp + grp + [_layer_spec(w_o.shape, layer)],
        out_specs=tok(d),
        out_shape=jax.ShapeDtypeStruct((n, d), jnp.float32),
        scratch_shapes=[pltpu.VMEM((2 * N_GROUPS, GROUP_WIDTH // 128, tm, 128), jnp.float32)],
        compiler_params=pltpu.CompilerParams(
            dimension_semantics=("arbitrary",), vmem_limit_bytes=VMEM_LIMIT_BYTES),
        name="attn_out_proj",
    )(x2d, *outs, *lses, w_o)


def _dilated_attention(x, layer, mix_layer, g, w_qkv, w_o, bias):
    b, s, d = x.shape
    x2d = x.reshape(b * s, d)
    views = _qkv_proj(x2d, layer, mix_layer, g, w_qkv)
    outs, lses = [], []
    for gi, (window, dil) in enumerate(DILATION_PAIRS):
        o, l = _attn_group(views[gi].reshape(b, s // dil, -1), bias, gi, dil, window // dil)
        outs.append(o.reshape(b * s // dil, dil * GROUP_WIDTH))
        lses.append(l.reshape(b * s // dil, dil * GROUP_WIDTH))
    return _attn_out_proj(x2d, outs, lses, layer, w_o).reshape(b, s, d)


def kernel(x, norm_mix, norm_ffn, final_norm, conv_w_pw1, conv_b_pw1, conv_w_dw, conv_b_dw, conv_ln_g, conv_ln_b, conv_w_pw2, conv_b_pw2, attn_w_qkv, attn_w_o, rel_bias, ffn_w_gate, ffn_w_up, ffn_w_down):
    b, s, d = x.shape
    depth = norm_mix.shape[0]
    n_back = DILATION_PAIRS[0][0] // DILATION_PAIRS[0][1]
    assert all(w // dl == n_back and s % w == 0 for w, dl in DILATION_PAIRS)
    bias = _bias_tables(rel_bias, n_back)
    w_pw1, w_pw2, w_qkv, w_o, w_gate, w_up, w_down = (
        _to_bf16(w) for w in (conv_w_pw1, conv_w_pw2, attn_w_qkv, attn_w_o, ffn_w_gate, ffn_w_up, ffn_w_down))
    rows = lambda v: v.reshape(v.shape[0], 1, v.shape[1])
    g_mix, g_ffn = rows(norm_mix), rows(norm_ffn)
    b_pw1, b_dw, ln_g, ln_b, b_pw2 = (rows(v) for v in (conv_b_pw1, conv_b_dw, conv_ln_g, conv_ln_b, conv_b_pw2))
    for i in range(depth):
        j = i // 2
        if i % 2 == 0:
            x = _conv_module(x, j, i, g_mix, w_pw1, b_pw1, conv_w_dw, b_dw, ln_g, ln_b, w_pw2, b_pw2)
        else:
            x = _dilated_attention(x, j, i, g_mix, w_qkv, w_o, bias)
        x = _ffn(x.reshape(b * s, d), i, g_ffn, w_gate, w_up, w_down, final_norm.reshape(1, d),
                 final_norm=(i == depth - 1)).reshape(b, s, d)
    return x
```

```python
import functools
import math

import numpy as np
import jax
import jax.numpy as jnp
from jax import lax
from jax.experimental import pallas as pl
from jax.experimental.pallas import tpu as pltpu

EPS = 1e-6
NEG_INF = -1e30
HEAD_DIM = 64
HEADS_PER_GROUP = 4
GROUP_WIDTH = HEADS_PER_GROUP * HEAD_DIM
DILATION_PAIRS = ((128, 1), (512, 4), (2048, 16))
N_GROUPS = len(DILATION_PAIRS)
N_BUCKETS = 32
REL_MAX_DISTANCE = 2048
CONV_WIDTH = 31
CONV_HALO = 32

VMEM_LIMIT_BYTES = 56 * 1024 * 1024
TOKEN_TILE = 512
ATTN_Q_BLOCKS = 4


def _const_spec(shape):
    nd = len(shape)
    return pl.BlockSpec(shape, lambda *_: (0,) * nd, pipeline_mode=pl.Buffered(1))


def _layer_spec(stack_shape, layer):
    return pl.BlockSpec((None,) + tuple(stack_shape[1:]), lambda *_: (layer, 0, 0), pipeline_mode=pl.Buffered(1))


def _cast_body(x_ref, o_ref):
    o_ref[...] = x_ref[...].astype(o_ref.dtype)


def _to_bf16(w):
    l, r, c = w.shape
    rows = 512
    assert (l * r) % rows == 0
    out = pl.pallas_call(
        _cast_body,
        grid=(l * r // rows,),
        in_specs=[pl.BlockSpec((rows, c), lambda i: (i, 0))],
        out_specs=pl.BlockSpec((rows, c), lambda i: (i, 0)),
        out_shape=jax.ShapeDtypeStruct((l * r, c), jnp.bfloat16),
        compiler_params=pltpu.CompilerParams(
            dimension_semantics=("arbitrary",), vmem_limit_bytes=VMEM_LIMIT_BYTES),
        name="to_bf16",
    )(w.reshape(l * r, c))
    return out.reshape(l, r, c)


def _rmsnorm_f32(x, g):
    return x * lax.rsqrt(jnp.mean(x * x, axis=-1, keepdims=True) + EPS) * g


def _mm(a, b):
    return jnp.dot(a, b, preferred_element_type=jnp.float32)


def _ffn_body(x_ref, g_ref, wg_ref, wu_ref, wd_ref, fg_ref, o_ref, h_ref, *, ff_chunk, final_norm):
    x = x_ref[...]
    xn = _rmsnorm_f32(x, g_ref[...]).astype(jnp.bfloat16)
    d_ff = wg_ref.shape[1]
    for c in range(d_ff // ff_chunk):
        cs = slice(c * ff_chunk, (c + 1) * ff_chunk)
        gate = _mm(xn, wg_ref[:, cs])
        up = _mm(xn, wu_ref[:, cs])
        h_ref[:, cs] = (gate * jax.nn.sigmoid(gate) * up).astype(jnp.bfloat16)
    y = x + _mm(h_ref[...], wd_ref[...])
    if final_norm:
        y = _rmsnorm_f32(y, fg_ref[...])
    o_ref[...] = y


def _ffn(x2d, layer, g, wg, wu, wd, fg, final_norm):
    n, d = x2d.shape
    d_ff = wg.shape[2]
    tm = TOKEN_TILE
    ff_chunk = 256
    assert n % tm == 0 and d_ff % ff_chunk == 0
    return pl.pallas_call(
        functools.partial(_ffn_body, ff_chunk=ff_chunk, final_norm=final_norm),
        grid=(n // tm,),
        in_specs=[pl.BlockSpec((tm, d), lambda i: (i, 0))]
        + [_layer_spec(a.shape, layer) for a in (g, wg, wu, wd)] + [_const_spec(fg.shape)],
        out_specs=pl.BlockSpec((tm, d), lambda i: (i, 0)),
        out_shape=jax.ShapeDtypeStruct((n, d), jnp.float32),
        scratch_shapes=[pltpu.VMEM((tm, d_ff), jnp.bfloat16)],
        compiler_params=pltpu.CompilerParams(
            dimension_semantics=("arbitrary",), vmem_limit_bytes=VMEM_LIMIT_BYTES),
        name="ffn",
    )(x2d, g, wg, wu, wd, fg)


def _conv_body(x_ref, g_ref, w1_ref, b1_ref, wdw_ref, bdw_ref, lng_ref, lnb_ref, w2_ref, b2_ref,
               o_ref, u_ref, c_ref, *, row_chunk):
    ts, d = x_ref.shape[1], x_ref.shape[2]
    c = w2_ref.shape[0]
    n_slab = c // 128

    @pl.when(pl.program_id(1) == 0)
    def _():
        u_ref[:, 0:CONV_HALO, :] = jnp.zeros((n_slab, CONV_HALO, 128), jnp.float32)

    x = x_ref[0]
    xn = _rmsnorm_f32(x, g_ref[...]).astype(jnp.bfloat16)
    base = CONV_HALO - (CONV_WIDTH - 1)
    mxu_n = 256
    for jp in range(c // mxu_n):
        cs = slice(jp * mxu_n, (jp + 1) * mxu_n)
        gs = slice(c + jp * mxu_n, c + (jp + 1) * mxu_n)
        a = _mm(xn, w1_ref[:, cs]) + b1_ref[:, cs]
        gate = _mm(xn, w1_ref[:, gs]) + b1_ref[:, gs]
        glu = a * jax.nn.sigmoid(gate)
        for half in range(mxu_n // 128):
            j = jp * (mxu_n // 128) + half
            ls = slice(j * 128, (j + 1) * 128)
            u_ref[j, CONV_HALO:CONV_HALO + ts, :] = glu[:, half * 128:(half + 1) * 128]
            for r0 in range(0, ts, row_chunk):
                acc = jnp.broadcast_to(bdw_ref[:, ls], (row_chunk, 128))
                for k in range(CONV_WIDTH):
                    acc = acc + wdw_ref[k:k + 1, ls] * u_ref[j, r0 + base + k:r0 + base + k + row_chunk, :]
                c_ref[r0:r0 + row_chunk, ls] = acc
            u_ref[j, 0:CONV_HALO, :] = u_ref[j, ts:ts + CONV_HALO, :]

    v = c_ref[...]
    mu = jnp.mean(v, axis=-1, keepdims=True)
    vc = v - mu
    var = jnp.mean(vc * vc, axis=-1, keepdims=True)
    y = vc * lax.rsqrt(var + EPS) * lng_ref[...] + lnb_ref[...]
    y = (y * jax.nn.sigmoid(y)).astype(jnp.bfloat16)
    o_ref[0] = x + _mm(y, w2_ref[...]) + b2_ref[...]


def _conv_module(x, layer, mix_layer, g, w1, b1, wdw, bdw, lng, lnb, w2, b2):
    b, s, d = x.shape
    c = w2.shape[1]
    ts = TOKEN_TILE
    assert s % ts == 0 and c % 128 == 0
    return pl.pallas_call(
        functools.partial(_conv_body, row_chunk=64),
        grid=(b, s // ts),
        in_specs=[pl.BlockSpec((1, ts, d), lambda bi, si: (bi, si, 0)), _layer_spec(g.shape, mix_layer)]
        + [_layer_spec(a.shape, layer) for a in (w1, b1, wdw, bdw, lng, lnb, w2, b2)],
        out_specs=pl.BlockSpec((1, ts, d), lambda bi, si: (bi, si, 0)),
        out_shape=jax.ShapeDtypeStruct((b, s, d), jnp.float32),
        scratch_shapes=[pltpu.VMEM((c // 128, ts + CONV_HALO, 128), jnp.float32),
                        pltpu.VMEM((ts, c), jnp.float32)],
        compiler_params=pltpu.CompilerParams(
            dimension_semantics=("arbitrary", "arbitrary"), vmem_limit_bytes=VMEM_LIMIT_BYTES),
        name="conv_module",
    )(x, g, w1, b1, wdw, bdw, lng, lnb, w2, b2)


def _t5_bucket_np(dist):
    max_exact = N_BUCKETS // 2
    n = np.maximum(dist, 0)
    nf = np.maximum(n, 1).astype(np.float32)
    large = max_exact + (np.log(nf / np.float32(max_exact)) / np.float32(math.log(REL_MAX_DISTANCE / max_exact))
                         * np.float32(N_BUCKETS - max_exact)).astype(np.int32)
    large = np.minimum(large, N_BUCKETS - 1)
    return np.where(n < max_exact, n, large).astype(np.int32)


def _bucket_index_tables(n_back):
    i_idx = np.arange(n_back)[:, None]
    j_idx = np.arange(2 * n_back)[None, :]
    dist = i_idx + n_back - j_idx
    valid = (dist >= 0) & (dist <= n_back)
    tabs = [np.where(valid, _t5_bucket_np(dist * dil), -1) for _, dil in DILATION_PAIRS]
    return np.stack(tabs).astype(np.int32)


def _bias_body(tab_ref, idx_ref, o_ref):
    g = pl.program_id(0)
    idx = idx_ref[0]
    for h in range(HEADS_PER_GROUP):
        acc = jnp.full(idx.shape, NEG_INF, jnp.float32)
        for bkt in range(N_BUCKETS):
            acc = jnp.where(idx == bkt, tab_ref[bkt, g * HEADS_PER_GROUP + h], acc)
        o_ref[0, h] = acc


def _bias_tables(rel_bias, n_back):
    idx = jnp.asarray(_bucket_index_tables(n_back))
    return pl.pallas_call(
        _bias_body,
        grid=(N_GROUPS,),
        in_specs=[pl.BlockSpec(memory_space=pltpu.SMEM),
                  pl.BlockSpec((1, n_back, 2 * n_back), lambda g: (g, 0, 0))],
        out_specs=pl.BlockSpec((1, HEADS_PER_GROUP, n_back, 2 * n_back), lambda g: (g, 0, 0, 0)),
        out_shape=jax.ShapeDtypeStruct((N_GROUPS, HEADS_PER_GROUP, n_back, 2 * n_back), jnp.float32),
        name="rel_bias_tables",
    )(rel_bias, idx)


def _qkv_body(x_ref, g_ref, w_ref, *refs, d_attn):
    o_refs, scr_ref = refs[:N_GROUPS], refs[N_GROUPS]
    tm = x_ref.shape[0]
    xn = _rmsnorm_f32(x_ref[...], g_ref[...]).astype(jnp.bfloat16)
    for gi, (_, dil) in enumerate(DILATION_PAIRS):
        for part in range(3):
            c0 = part * d_attn + gi * GROUP_WIDTH
            y = _mm(xn, w_ref[:, c0:c0 + GROUP_WIDTH])
            if part == 0:
                y = y * (HEAD_DIM ** -0.5)
            if dil == 1:
                o_refs[gi][:, part * GROUP_WIDTH:(part + 1) * GROUP_WIDTH] = y.astype(jnp.bfloat16)
                continue
            for half in range(GROUP_WIDTH // 128):
                scr_ref[half] = y[:, half * 128:(half + 1) * 128]
            for r in range(dil):
                for half in range(GROUP_WIDTH // 128):
                    lane0 = (r * 3 + part) * GROUP_WIDTH + half * 128
                    o_refs[gi][:, lane0:lane0 + 128] = (
                        scr_ref[half, pl.ds(r, tm // dil, stride=dil), :].astype(jnp.bfloat16))


def _qkv_proj(x2d, layer, mix_layer, g, w):
    n, d = x2d.shape
    dq = w.shape[2]
    tm = TOKEN_TILE
    gw3 = 3 * GROUP_WIDTH
    return pl.pallas_call(
        functools.partial(_qkv_body, d_attn=dq // 3),
        grid=(n // tm,),
        in_specs=[pl.BlockSpec((tm, d), lambda i: (i, 0)), _layer_spec(g.shape, mix_layer),
                  _layer_spec(w.shape, layer)],
        out_specs=[pl.BlockSpec((tm // dil, dil * gw3), lambda i: (i, 0)) for _, dil in DILATION_PAIRS],
        out_shape=[jax.ShapeDtypeStruct((n // dil, dil * gw3), jnp.bfloat16) for _, dil in DILATION_PAIRS],
        scratch_shapes=[pltpu.VMEM((GROUP_WIDTH // 128, tm, 128), jnp.float32)],
        compiler_params=pltpu.CompilerParams(
            dimension_semantics=("arbitrary",), vmem_limit_bytes=VMEM_LIMIT_BYTES),
        name="qkv_proj",
    )(x2d, g, w)


def _attn_body(q_ref, kp_ref, kc_ref, vp_ref, vc_ref, bias_ref, o_ref, l_ref, *, n_back):
    first = pl.program_id(2) == 0
    tq = q_ref.shape[1]
    for h in range(HEADS_PER_GROUP):
        hs = slice(h * HEAD_DIM, (h + 1) * HEAD_DIM)
        bias = bias_ref[0, h]
        for i in range(tq // n_back):
            rows = slice(i * n_back, (i + 1) * n_back)
            q = q_ref[0, rows, hs]
            if i == 0:
                k = jnp.concatenate([kp_ref[0, :, hs], kc_ref[0, 0:n_back, hs]], axis=0)
                v = jnp.concatenate([vp_ref[0, :, hs], vc_ref[0, 0:n_back, hs]], axis=0)
            else:
                k = kc_ref[0, (i - 1) * n_back:(i + 1) * n_back, hs]
                v = vc_ref[0, (i - 1) * n_back:(i + 1) * n_back, hs]
            logits = lax.dot_general(q, k, (((1,), (1,)), ((), ())),
                                     preferred_element_type=jnp.float32) + bias
            if i == 0:
                col = lax.broadcasted_iota(jnp.int32, logits.shape, 1)
                logits = jnp.where(jnp.logical_and(first, col < n_back), NEG_INF, logits)
            m = jnp.max(logits, axis=-1, keepdims=True)
            p = jnp.exp(logits - m)
            den = jnp.sum(p, axis=-1, keepdims=True)
            o = _mm(p.astype(jnp.bfloat16), v) / den
            o_ref[0, rows, hs] = o
            l_ref[0, rows, hs] = jnp.broadcast_to(m + jnp.log(den), (n_back, HEAD_DIM))


def _attn_group(view, bias, g, dilation, n_back):
    b, sub, _ = view.shape
    tq = min(ATTN_Q_BLOCKS * n_back, sub)
    assert sub % tq == 0
    nprev = tq // n_back
    q_col, k_col, v_col = 0, 1, 2

    def cur(off):
        return pl.BlockSpec((1, tq, GROUP_WIDTH), lambda bi, r, n: (bi, n, r * 3 + off))

    def prev(off):
        return pl.BlockSpec((1, n_back, GROUP_WIDTH),
                            lambda bi, r, n: (bi, jnp.maximum(n * nprev - 1, 0), r * 3 + off))

    out_spec = pl.BlockSpec((1, tq, GROUP_WIDTH), lambda bi, r, n: (bi, n, r))
    out_sds = jax.ShapeDtypeStruct((b, sub, dilation * GROUP_WIDTH), jnp.float32)
    return pl.pallas_call(
        functools.partial(_attn_body, n_back=n_back),
        grid=(b, dilation, sub // tq),
        in_specs=[cur(q_col), prev(k_col), cur(k_col), prev(v_col), cur(v_col),
                  pl.BlockSpec((1, HEADS_PER_GROUP, n_back, 2 * n_back), lambda bi, r, n: (g, 0, 0, 0))],
        out_specs=[out_spec, out_spec],
        out_shape=[out_sds, out_sds],
        compiler_params=pltpu.CompilerParams(
            dimension_semantics=("arbitrary", "arbitrary", "arbitrary"), vmem_limit_bytes=VMEM_LIMIT_BYTES),
        name=f"dilated_attn_g{g}",
    )(view, view, view, view, view, bias)


def _attn_out_body(x_ref, o0_ref, o1_ref, o2_ref, l0_ref, l1_ref, l2_ref, w_ref, y_ref, scr_ref):
    tm = x_ref.shape[0]
    halves = GROUP_WIDTH // 128

    def token_major(ref, gi, slot):
        dil = DILATION_PAIRS[gi][1]
        if dil == 1:
            return ref[...]
        for r in range(dil):
            for half in range(halves):
                lane0 = r * GROUP_WIDTH + half * 128
                scr_ref[slot, half, pl.ds(r, tm // dil, stride=dil), :] = ref[:, lane0:lane0 + 128]
        return jnp.concatenate([scr_ref[slot, half] for half in range(halves)], axis=1)

    o_refs, l_refs = (o0_ref, o1_ref, o2_ref), (l0_ref, l1_ref, l2_ref)
    ls = [token_major(l_refs[gi], gi, gi) for gi in range(N_GROUPS)]
    mx = jnp.maximum(jnp.maximum(ls[0], ls[1]), ls[2])
    es = [jnp.exp(l - mx) for l in ls]
    tot = es[0] + es[1] + es[2]
    y = x_ref[...]
    for gi in range(N_GROUPS):
        og = (token_major(o_refs[gi], gi, N_GROUPS + gi) * (es[gi] / tot)).astype(jnp.bfloat16)
        y = y + _mm(og, w_ref[gi * GROUP_WIDTH:(gi + 1) * GROUP_WIDTH, :])
    y_ref[...] = y


def _attn_out_proj(x2d, outs, lses, layer, w_o):
    n, d = x2d.shape
    tm = TOKEN_TILE
    tok = lambda w: pl.BlockSpec((tm, w), lambda i: (i, 0))
    grp = [pl.BlockSpec((tm // dil, dil * GROUP_WIDTH), lambda i: (i, 0)) for _, dil in DILATION_PAIRS]
    return pl.pallas_call(
        _attn_out_body,
        grid=(n // tm,),
        in_specs=[tok(d)] + grp + grp + [_layer_spec(w_o.shape, layer)],
        out_specs=tok(d),
        out_shape=jax.ShapeDtypeStruct((n, d), jnp.float32),
        scratch_shapes=[pltpu.VMEM((2 * N_GROUPS, GROUP_WIDTH // 128, tm, 128), jnp.float32)],
        compiler_params=pltpu.CompilerParams(
            dimension_semantics=("arbitrary",), vmem_limit_bytes=VMEM_LIMIT_BYTES),
        name="attn_out_proj",
    )(x2d, *outs, *lses, w_o)


def _dilated_attention(x, layer, mix_layer, g, w_qkv, w_o, bias):
    b, s, d = x.shape
    x2d = x.reshape(b * s, d)
    views = _qkv_proj(x2d, layer, mix_layer, g, w_qkv)
    outs, lses = [], []
    for gi, (window, dil) in enumerate(DILATION_PAIRS):
        o, l = _attn_group(views[gi].reshape(b, s // dil, -1), bias, gi, dil, window // dil)
        outs.append(o.reshape(b * s // dil, dil * GROUP_WIDTH))
        lses.append(l.reshape(b * s // dil, dil * GROUP_WIDTH))
    return _attn_out_proj(x2d, outs, lses, layer, w_o).reshape(b, s, d)


def kernel(x, norm_mix, norm_ffn, final_norm, conv_w_pw1, conv_b_pw1, conv_w_dw, conv_b_dw, conv_ln_g, conv_ln_b, conv_w_pw2, conv_b_pw2, attn_w_qkv, attn_w_o, rel_bias, ffn_w_gate, ffn_w_up, ffn_w_down):
    b, s, d = x.shape
    depth = norm_mix.shape[0]
    n_back = DILATION_PAIRS[0][0] // DILATION_PAIRS[0][1]
    assert all(w // dl == n_back and s % w == 0 for w, dl in DILATION_PAIRS)
    bias = _bias_tables(rel_bias, n_back)
    w_pw1, w_pw2, w_qkv, w_o, w_gate, w_up, w_down = (
        _to_bf16(w) for w in (conv_w_pw1, conv_w_pw2, attn_w_qkv, attn_w_o, ffn_w_gate, ffn_w_up, ffn_w_down))
    rows = lambda v: v.reshape(v.shape[0], 1, v.shape[1])
    g_mix, g_ffn = rows(norm_mix), rows(norm_ffn)
    b_pw1, b_dw, ln_g, ln_b, b_pw2 = (rows(v) for v in (conv_b_pw1, conv_b_dw, conv_ln_g, conv_ln_b, conv_b_pw2))
    for i in range(depth):
        j = i // 2
        if i % 2 == 0:
            x = _conv_module(x, j, i, g_mix, w_pw1, b_pw1, conv_w_dw, b_dw, ln_g, ln_b, w_pw2, b_pw2)
        else:
            x = _dilated_attention(x, j, i, g_mix, w_qkv, w_o, bias)
        x = _ffn(x.reshape(b * s, d), i, g_ffn, w_gate, w_up, w_down, final_norm.reshape(1, d),
                 final_norm=(i == depth - 1)).reshape(b, s, d)
    return x
```

```python
import functools
import math

import numpy as np
import jax
import jax.numpy as jnp
from jax import lax
from jax.experimental import pallas as pl
from jax.experimental.pallas import tpu as pltpu

EPS = 1e-6
NEG_INF = -1e30
HEAD_DIM = 64
HEADS_PER_GROUP = 4
GROUP_WIDTH = HEADS_PER_GROUP * HEAD_DIM
DILATION_PAIRS = ((128, 1), (512, 4), (2048, 16))
N_GROUPS = len(DILATION_PAIRS)
N_BUCKETS = 32
REL_MAX_DISTANCE = 2048
CONV_WIDTH = 31
CONV_HALO = 32

VMEM_LIMIT_BYTES = 56 * 1024 * 1024
TOKEN_TILE = 512
ATTN_Q_BLOCKS = 8


def _const_spec(shape):
    nd = len(shape)
    return pl.BlockSpec(shape, lambda *_: (0,) * nd, pipeline_mode=pl.Buffered(1))


def _layer_spec(stack_shape, layer):
    return pl.BlockSpec((None,) + tuple(stack_shape[1:]), lambda *_: (layer, 0, 0), pipeline_mode=pl.Buffered(1))


def _cast_body(x_ref, o_ref):
    o_ref[...] = x_ref[...].astype(o_ref.dtype)


def _to_bf16(w):
    l, r, c = w.shape
    rows = 512
    assert (l * r) % rows == 0
    out = pl.pallas_call(
        _cast_body,
        grid=(l * r // rows,),
        in_specs=[pl.BlockSpec((rows, c), lambda i: (i, 0))],
        out_specs=pl.BlockSpec((rows, c), lambda i: (i, 0)),
        out_shape=jax.ShapeDtypeStruct((l * r, c), jnp.bfloat16),
        compiler_params=pltpu.CompilerParams(
            dimension_semantics=("arbitrary",), vmem_limit_bytes=VMEM_LIMIT_BYTES),
        name="to_bf16",
    )(w.reshape(l * r, c))
    return out.reshape(l, r, c)


def _rmsnorm_f32(x, g):
    return x * lax.rsqrt(jnp.mean(x * x, axis=-1, keepdims=True) + EPS) * g


def _mm(a, b):
    return jnp.dot(a, b, preferred_element_type=jnp.float32)


FF_CHUNK = 256


def _swiglu_residual(x, g_ref, wg_ref, wu_ref, wd_ref, fg_ref, h_ref, final_norm):
    xn = _rmsnorm_f32(x, g_ref[...]).astype(jnp.bfloat16)
    for c in range(wg_ref.shape[1] // FF_CHUNK):
        cs = slice(c * FF_CHUNK, (c + 1) * FF_CHUNK)
        gate = _mm(xn, wg_ref[:, cs])
        up = _mm(xn, wu_ref[:, cs])
        h_ref[:, cs] = (gate * jax.nn.sigmoid(gate) * up).astype(jnp.bfloat16)
    y = x + _mm(h_ref[...], wd_ref[...])
    if final_norm:
        y = _rmsnorm_f32(y, fg_ref[...])
    return y


def _ffn_body(x_ref, g_ref, wg_ref, wu_ref, wd_ref, fg_ref, o_ref, h_ref, *, final_norm):
    o_ref[...] = _swiglu_residual(x_ref[...], g_ref, wg_ref, wu_ref, wd_ref, fg_ref, h_ref, final_norm)


def _ffn(x2d, layer, g, wg, wu, wd, fg, final_norm):
    n, d = x2d.shape
    d_ff = wg.shape[2]
    tm = 2 * TOKEN_TILE
    assert n % tm == 0 and d_ff % FF_CHUNK == 0
    return pl.pallas_call(
        functools.partial(_ffn_body, final_norm=final_norm),
        grid=(n // tm,),
        in_specs=[pl.BlockSpec((tm, d), lambda i: (i, 0))]
        + [_layer_spec(a.shape, layer) for a in (g, wg, wu, wd)] + [_const_spec(fg.shape)],
        out_specs=pl.BlockSpec((tm, d), lambda i: (i, 0)),
        out_shape=jax.ShapeDtypeStruct((n, d), jnp.float32),
        scratch_shapes=[pltpu.VMEM((tm, d_ff), jnp.bfloat16)],
        compiler_params=pltpu.CompilerParams(
            dimension_semantics=("arbitrary",), vmem_limit_bytes=VMEM_LIMIT_BYTES),
        name="ffn",
    )(x2d, g, wg, wu, wd, fg)


def _conv_body(x_ref, g_ref, w1_ref, b1_ref, wdw_ref, bdw_ref, lng_ref, lnb_ref, w2_ref, b2_ref,
               o_ref, u_ref, c_ref, *, row_chunk):
    ts, d = x_ref.shape[1], x_ref.shape[2]
    c = w2_ref.shape[0]
    n_slab = c // 128

    @pl.when(pl.program_id(1) == 0)
    def _():
        u_ref[:, 0:CONV_HALO, :] = jnp.zeros((n_slab, CONV_HALO, 128), jnp.float32)

    x = x_ref[0]
    xn = _rmsnorm_f32(x, g_ref[...]).astype(jnp.bfloat16)
    base = CONV_HALO - (CONV_WIDTH - 1)
    mxu_n = 256
    for jp in range(c // mxu_n):
        cs = slice(jp * mxu_n, (jp + 1) * mxu_n)
        gs = slice(c + jp * mxu_n, c + (jp + 1) * mxu_n)
        a = _mm(xn, w1_ref[:, cs]) + b1_ref[:, cs]
        gate = _mm(xn, w1_ref[:, gs]) + b1_ref[:, gs]
        glu = a * jax.nn.sigmoid(gate)
        for half in range(mxu_n // 128):
            j = jp * (mxu_n // 128) + half
            ls = slice(j * 128, (j + 1) * 128)
            u_ref[j, CONV_HALO:CONV_HALO + ts, :] = glu[:, half * 128:(half + 1) * 128]
            for r0 in range(0, ts, row_chunk):
                acc = jnp.broadcast_to(bdw_ref[:, ls], (row_chunk, 128))
                for k in range(CONV_WIDTH):
                    acc = acc + wdw_ref[k:k + 1, ls] * u_ref[j, r0 + base + k:r0 + base + k + row_chunk, :]
                c_ref[r0:r0 + row_chunk, ls] = acc
            u_ref[j, 0:CONV_HALO, :] = u_ref[j, ts:ts + CONV_HALO, :]

    v = c_ref[...]
    mu = jnp.mean(v, axis=-1, keepdims=True)
    vc = v - mu
    var = jnp.mean(vc * vc, axis=-1, keepdims=True)
    y = vc * lax.rsqrt(var + EPS) * lng_ref[...] + lnb_ref[...]
    y = (y * jax.nn.sigmoid(y)).astype(jnp.bfloat16)
    o_ref[0] = x + _mm(y, w2_ref[...]) + b2_ref[...]


def _conv_module(x, layer, mix_layer, g, w1, b1, wdw, bdw, lng, lnb, w2, b2):
    b, s, d = x.shape
    c = w2.shape[1]
    ts = TOKEN_TILE
    assert s % ts == 0 and c % 128 == 0
    return pl.pallas_call(
        functools.partial(_conv_body, row_chunk=64),
        grid=(b, s // ts),
        in_specs=[pl.BlockSpec((1, ts, d), lambda bi, si: (bi, si, 0)), _layer_spec(g.shape, mix_layer)]
        + [_layer_spec(a.shape, layer) for a in (w1, b1, wdw, bdw, lng, lnb, w2, b2)],
        out_specs=pl.BlockSpec((1, ts, d), lambda bi, si: (bi, si, 0)),
        out_shape=jax.ShapeDtypeStruct((b, s, d), jnp.float32),
        scratch_shapes=[pltpu.VMEM((c // 128, ts + CONV_HALO, 128), jnp.float32),
                        pltpu.VMEM((ts, c), jnp.float32)],
        compiler_params=pltpu.CompilerParams(
            dimension_semantics=("arbitrary", "arbitrary"), vmem_limit_bytes=VMEM_LIMIT_BYTES),
        name="conv_module",
    )(x, g, w1, b1, wdw, bdw, lng, lnb, w2, b2)


def _t5_bucket_np(dist):
    max_exact = N_BUCKETS // 2
    n = np.maximum(dist, 0)
    nf = np.maximum(n, 1).astype(np.float32)
    large = max_exact + (np.log(nf / np.float32(max_exact)) / np.float32(math.log(REL_MAX_DISTANCE / max_exact))
                         * np.float32(N_BUCKETS - max_exact)).astype(np.int32)
    large = np.minimum(large, N_BUCKETS - 1)
    return np.where(n < max_exact, n, large).astype(np.int32)


def _bucket_index_tables(n_back):
    i_idx = np.arange(n_back)[:, None]
    j_idx = np.arange(2 * n_back)[None, :]
    dist = i_idx + n_back - j_idx
    valid = (dist >= 0) & (dist <= n_back)
    tabs = [np.where(valid, _t5_bucket_np(dist * dil), -1) for _, dil in DILATION_PAIRS]
    return np.stack(tabs).astype(np.int32)


def _bias_body(tab_ref, idx_ref, o_ref):
    g = pl.program_id(0)
    idx = idx_ref[0]
    for h in range(HEADS_PER_GROUP):
        acc = jnp.full(idx.shape, NEG_INF, jnp.float32)
        for bkt in range(N_BUCKETS):
            acc = jnp.where(idx == bkt, tab_ref[bkt, g * HEADS_PER_GROUP + h], acc)
        o_ref[0, h] = acc


def _bias_tables(rel_bias, n_back):
    idx = jnp.asarray(_bucket_index_tables(n_back))
    return pl.pallas_call(
        _bias_body,
        grid=(N_GROUPS,),
        in_specs=[pl.BlockSpec(memory_space=pltpu.SMEM),
                  pl.BlockSpec((1, n_back, 2 * n_back), lambda g: (g, 0, 0))],
        out_specs=pl.BlockSpec((1, HEADS_PER_GROUP, n_back, 2 * n_back), lambda g: (g, 0, 0, 0)),
        out_shape=jax.ShapeDtypeStruct((N_GROUPS, HEADS_PER_GROUP, n_back, 2 * n_back), jnp.float32),
        name="rel_bias_tables",
    )(rel_bias, idx)


def _qkv_body(x_ref, g_ref, w_ref, *refs, d_attn):
    o_refs, scr_ref = refs[:N_GROUPS], refs[N_GROUPS]
    tm = x_ref.shape[0]
    xn = _rmsnorm_f32(x_ref[...], g_ref[...]).astype(jnp.bfloat16)
    for gi, (_, dil) in enumerate(DILATION_PAIRS):
        for part in range(3):
            c0 = part * d_attn + gi * GROUP_WIDTH
            y = _mm(xn, w_ref[:, c0:c0 + GROUP_WIDTH])
            if part == 0:
                y = y * (HEAD_DIM ** -0.5)
            if dil == 1:
                o_refs[gi][:, part * GROUP_WIDTH:(part + 1) * GROUP_WIDTH] = y.astype(jnp.bfloat16)
                continue
            for half in range(GROUP_WIDTH // 128):
                scr_ref[half] = y[:, half * 128:(half + 1) * 128]
            for r in range(dil):
                for half in range(GROUP_WIDTH // 128):
                    lane0 = (r * 3 + part) * GROUP_WIDTH + half * 128
                    o_refs[gi][:, lane0:lane0 + 128] = (
                        scr_ref[half, pl.ds(r, tm // dil, stride=dil), :].astype(jnp.bfloat16))


def _qkv_proj(x2d, layer, mix_layer, g, w):
    n, d = x2d.shape
    dq = w.shape[2]
    tm = TOKEN_TILE
    gw3 = 3 * GROUP_WIDTH
    return pl.pallas_call(
        functools.partial(_qkv_body, d_attn=dq // 3),
        grid=(n // tm,),
        in_specs=[pl.BlockSpec((tm, d), lambda i: (i, 0)), _layer_spec(g.shape, mix_layer),
                  _layer_spec(w.shape, layer)],
        out_specs=[pl.BlockSpec((tm // dil, dil * gw3), lambda i: (i, 0)) for _, dil in DILATION_PAIRS],
        out_shape=[jax.ShapeDtypeStruct((n // dil, dil * gw3), jnp.bfloat16) for _, dil in DILATION_PAIRS],
        scratch_shapes=[pltpu.VMEM((GROUP_WIDTH // 128, tm, 128), jnp.float32)],
        compiler_params=pltpu.CompilerParams(
            dimension_semantics=("arbitrary",), vmem_limit_bytes=VMEM_LIMIT_BYTES),
        name="qkv_proj",
    )(x2d, g, w)


def _attn_body(q_ref, kp_ref, kc_ref, vp_ref, vc_ref, bias_ref, o_ref, l_ref, *, n_back):
    first = pl.program_id(2) == 0
    tq = q_ref.shape[1]
    pair_w = 2 * HEAD_DIM
    lane = lax.broadcasted_iota(jnp.int32, (n_back, pair_w), 1)
    low = lane < HEAD_DIM
    kv_low = lax.broadcasted_iota(jnp.int32, (2 * n_back, pair_w), 1) < HEAD_DIM
    for hp in range(HEADS_PER_GROUP // 2):
        ps = slice(hp * pair_w, (hp + 1) * pair_w)
        bias = bias_ref[0, 2 * hp:2 * hp + 2].reshape(2 * n_back, 2 * n_back)
        for i in range(tq // n_back):
            rows = slice(i * n_back, (i + 1) * n_back)
            q = q_ref[0, rows, ps]
            if i == 0:
                k = jnp.concatenate([kp_ref[0, :, ps], kc_ref[0, 0:n_back, ps]], axis=0)
                v = jnp.concatenate([vp_ref[0, :, ps], vc_ref[0, 0:n_back, ps]], axis=0)
            else:
                k = kc_ref[0, (i - 1) * n_back:(i + 1) * n_back, ps]
                v = vc_ref[0, (i - 1) * n_back:(i + 1) * n_back, ps]
            zero = jnp.zeros_like(q)
            q2 = jnp.concatenate([jnp.where(low, q, zero), jnp.where(low, zero, q)], axis=0)
            logits = lax.dot_general(q2, k, (((1,), (1,)), ((), ())),
                                     preferred_element_type=jnp.float32) + bias
            if i == 0:
                col = lax.broadcasted_iota(jnp.int32, logits.shape, 1)
                logits = jnp.where(jnp.logical_and(first, col < n_back), NEG_INF, logits)
            m = jnp.max(logits, axis=-1, keepdims=True)
            p = jnp.exp(logits - m)
            den = jnp.sum(p, axis=-1, keepdims=True)
            pb = p.astype(jnp.bfloat16)
            zv = jnp.zeros_like(v)
            v2 = jnp.concatenate([jnp.where(kv_low, v, zv), jnp.where(kv_low, zv, v)], axis=0)
            p2 = jnp.concatenate([pb[:n_back], pb[n_back:]], axis=1)
            o = _mm(p2, v2)
            lse = m + jnp.log(den)
            o_ref[0, rows, ps] = o / jnp.where(low, den[:n_back], den[n_back:])
            l_ref[0, rows, ps] = jnp.where(low, lse[:n_back], lse[n_back:])


def _attn_group(view, bias, g, dilation, n_back):
    b, sub, _ = view.shape
    tq = min(ATTN_Q_BLOCKS * n_back, sub)
    assert sub % tq == 0
    nprev = tq // n_back
    q_col, k_col, v_col = 0, 1, 2

    def cur(off):
        return pl.BlockSpec((1, tq, GROUP_WIDTH), lambda bi, r, n: (bi, n, r * 3 + off))

    def prev(off):
        return pl.BlockSpec((1, n_back, GROUP_WIDTH),
                            lambda bi, r, n: (bi, jnp.maximum(n * nprev - 1, 0), r * 3 + off))

    out_spec = pl.BlockSpec((1, tq, GROUP_WIDTH), lambda bi, r, n: (bi, n, r))
    out_sds = jax.ShapeDtypeStruct((b, sub, dilation * GROUP_WIDTH), jnp.float32)
    return pl.pallas_call(
        functools.partial(_attn_body, n_back=n_back),
        grid=(b, dilation, sub // tq),
        in_specs=[cur(q_col), prev(k_col), cur(k_col), prev(v_col), cur(v_col),
                  pl.BlockSpec((1, HEADS_PER_GROUP, n_back, 2 * n_back), lambda bi, r, n: (g, 0, 0, 0))],
        out_specs=[out_spec, out_spec],
        out_shape=[out_sds, out_sds],
        compiler_params=pltpu.CompilerParams(
            dimension_semantics=("arbitrary", "arbitrary", "arbitrary"), vmem_limit_bytes=VMEM_LIMIT_BYTES),
        name=f"dilated_attn_g{g}",
    )(view, view, view, view, view, bias)


def _attn_out_ffn_body(x_ref, o0_ref, o1_ref, o2_ref, l0_ref, l1_ref, l2_ref, wo_ref,
                       g_ref, wg_ref, wu_ref, wd_ref, fg_ref, y_ref, scr_ref, h_ref, *, final_norm):
    tm = x_ref.shape[0]
    halves = GROUP_WIDTH // 128

    def token_major(ref, gi, slot):
        dil = DILATION_PAIRS[gi][1]
        if dil == 1:
            return ref[...]
        for r in range(dil):
            for half in range(halves):
                lane0 = r * GROUP_WIDTH + half * 128
                scr_ref[slot, half, pl.ds(r, tm // dil, stride=dil), :] = ref[:, lane0:lane0 + 128]
        return jnp.concatenate([scr_ref[slot, half] for half in range(halves)], axis=1)

    o_refs, l_refs = (o0_ref, o1_ref, o2_ref), (l0_ref, l1_ref, l2_ref)
    ls = [token_major(l_refs[gi], gi, gi) for gi in range(N_GROUPS)]
    mx = jnp.maximum(jnp.maximum(ls[0], ls[1]), ls[2])
    es = [jnp.exp(l - mx) for l in ls]
    tot = es[0] + es[1] + es[2]
    x = x_ref[...]
    for gi in range(N_GROUPS):
        og = (token_major(o_refs[gi], gi, N_GROUPS + gi) * (es[gi] / tot)).astype(jnp.bfloat16)
        x = x + _mm(og, wo_ref[gi * GROUP_WIDTH:(gi + 1) * GROUP_WIDTH, :])
    y_ref[...] = _swiglu_residual(x, g_ref, wg_ref, wu_ref, wd_ref, fg_ref, h_ref, final_norm)


def _attn_out_ffn(x2d, outs, lses, attn_layer, w_o, layer, g, wg, wu, wd, fg, final_norm):
    n, d = x2d.shape
    d_ff = wg.shape[2]
    tm = TOKEN_TILE
    tok = lambda w: pl.BlockSpec((tm, w), lambda i: (i, 0))
    grp = [pl.BlockSpec((tm // dil, dil * GROUP_WIDTH), lambda i: (i, 0)) for _, dil in DILATION_PAIRS]
    return pl.pallas_call(
        functools.partial(_attn_out_ffn_body, final_norm=final_norm),
        grid=(n // tm,),
        in_specs=[tok(d)] + grp + grp + [_layer_spec(w_o.shape, attn_layer)]
        + [_layer_spec(a.shape, layer) for a in (g, wg, wu, wd)] + [_const_spec(fg.shape)],
        out_specs=tok(d),
        out_shape=jax.ShapeDtypeStruct((n, d), jnp.float32),
        scratch_shapes=[pltpu.VMEM((2 * N_GROUPS, GROUP_WIDTH // 128, tm, 128), jnp.float32),
                        pltpu.VMEM((tm, d_ff), jnp.bfloat16)],
        compiler_params=pltpu.CompilerParams(
            dimension_semantics=("arbitrary",), vmem_limit_bytes=VMEM_LIMIT_BYTES),
        name="attn_out_ffn",
    )(x2d, *outs, *lses, w_o, g, wg, wu, wd, fg)


def _dilated_attention(x2d, b, layer, mix_layer, g, w_qkv, bias):
    n = x2d.shape[0]
    views = _qkv_proj(x2d, layer, mix_layer, g, w_qkv)
    outs, lses = [], []
    for gi, (window, dil) in enumerate(DILATION_PAIRS):
        o, l = _attn_group(views[gi].reshape(b, n // b // dil, -1), bias, gi, dil, window // dil)
        outs.append(o.reshape(n // dil, dil * GROUP_WIDTH))
        lses.append(l.reshape(n // dil, dil * GROUP_WIDTH))
    return outs, lses


def kernel(x, norm_mix, norm_ffn, final_norm, conv_w_pw1, conv_b_pw1, conv_w_dw, conv_b_dw, conv_ln_g, conv_ln_b, conv_w_pw2, conv_b_pw2, attn_w_qkv, attn_w_o, rel_bias, ffn_w_gate, ffn_w_up, ffn_w_down):
    b, s, d = x.shape
    depth = norm_mix.shape[0]
    n_back = DILATION_PAIRS[0][0] // DILATION_PAIRS[0][1]
    assert all(w // dl == n_back and s % w == 0 for w, dl in DILATION_PAIRS)
    bias = _bias_tables(rel_bias, n_back)
    w_pw1, w_pw2, w_qkv, w_o, w_gate, w_up, w_down = (
        _to_bf16(w) for w in (conv_w_pw1, conv_w_pw2, attn_w_qkv, attn_w_o, ffn_w_gate, ffn_w_up, ffn_w_down))
    rows = lambda v: v.reshape(v.shape[0], 1, v.shape[1])
    g_mix, g_ffn = rows(norm_mix), rows(norm_ffn)
    b_pw1, b_dw, ln_g, ln_b, b_pw2 = (rows(v) for v in (conv_b_pw1, conv_b_dw, conv_ln_g, conv_ln_b, conv_b_pw2))
    fg = final_norm.reshape(1, d)
    for i in range(depth):
        j = i // 2
        last = i == depth - 1
        if i % 2 == 0:
            x = _conv_module(x, j, i, g_mix, w_pw1, b_pw1, conv_w_dw, b_dw, ln_g, ln_b, w_pw2, b_pw2)
            x = _ffn(x.reshape(b * s, d), i, g_ffn, w_gate, w_up, w_down, fg, last).reshape(b, s, d)
        else:
            x2d = x.reshape(b * s, d)
            outs, lses = _dilated_attention(x2d, b, j, i, g_mix, w_qkv, bias)
            x = _attn_out_ffn(x2d, outs, lses, j, w_o, i, g_ffn, w_gate, w_up, w_down, fg, last).reshape(b, s, d)
    return x
```

```python
import functools
import math

import numpy as np
import jax
import jax.numpy as jnp
from jax import lax
from jax.experimental import pallas as pl
from jax.experimental.pallas import tpu as pltpu

EPS = 1e-6
NEG_INF = -1e30
HEAD_DIM = 64
HEADS_PER_GROUP = 4
GROUP_WIDTH = HEADS_PER_GROUP * HEAD_DIM
DILATION_PAIRS = ((128, 1), (512, 4), (2048, 16))
N_GROUPS = len(DILATION_PAIRS)
N_BUCKETS = 32
REL_MAX_DISTANCE = 2048
CONV_WIDTH = 31
CONV_HALO = 32

VMEM_LIMIT_BYTES = 56 * 1024 * 1024
TOKEN_TILE = 512
ATTN_Q_BLOCKS = 8


def _const_spec(shape):
    nd = len(shape)
    return pl.BlockSpec(shape, lambda *_: (0,) * nd, pipeline_mode=pl.Buffered(1))


def _layer_spec(stack_shape, layer):
    return pl.BlockSpec((None,) + tuple(stack_shape[1:]), lambda *_: (layer, 0, 0), pipeline_mode=pl.Buffered(1))


def _cast_body(x_ref, o_ref):
    o_ref[...] = x_ref[...].astype(o_ref.dtype)


def _to_bf16(w):
    l, r, c = w.shape
    rows = 512
    assert (l * r) % rows == 0
    out = pl.pallas_call(
        _cast_body,
        grid=(l * r // rows,),
        in_specs=[pl.BlockSpec((rows, c), lambda i: (i, 0))],
        out_specs=pl.BlockSpec((rows, c), lambda i: (i, 0)),
        out_shape=jax.ShapeDtypeStruct((l * r, c), jnp.bfloat16),
        compiler_params=pltpu.CompilerParams(
            dimension_semantics=("arbitrary",), vmem_limit_bytes=VMEM_LIMIT_BYTES),
        name="to_bf16",
    )(w.reshape(l * r, c))
    return out.reshape(l, r, c)


def _rmsnorm_f32(x, g):
    return x * lax.rsqrt(jnp.mean(x * x, axis=-1, keepdims=True) + EPS) * g


def _mm(a, b):
    return jnp.dot(a, b, preferred_element_type=jnp.float32)


FF_CHUNK = 256


def _swiglu_residual(x, g_ref, wg_ref, wu_ref, wd_ref, fg_ref, h_ref, final_norm):
    xn = _rmsnorm_f32(x, g_ref[...]).astype(jnp.bfloat16)
    for c in range(wg_ref.shape[1] // FF_CHUNK):
        cs = slice(c * FF_CHUNK, (c + 1) * FF_CHUNK)
        gate = _mm(xn, wg_ref[:, cs])
        up = _mm(xn, wu_ref[:, cs])
        h_ref[:, cs] = (gate * jax.nn.sigmoid(gate) * up).astype(jnp.bfloat16)
    y = x + _mm(h_ref[...], wd_ref[...])
    if final_norm:
        y = _rmsnorm_f32(y, fg_ref[...])
    return y


def _ffn_body(x_ref, g_ref, wg_ref, wu_ref, wd_ref, fg_ref, o_ref, h_ref, *, final_norm):
    o_ref[...] = _swiglu_residual(x_ref[...], g_ref, wg_ref, wu_ref, wd_ref, fg_ref, h_ref, final_norm)


def _ffn(x2d, layer, g, wg, wu, wd, fg, final_norm):
    n, d = x2d.shape
    d_ff = wg.shape[2]
    tm = 2 * TOKEN_TILE
    assert n % tm == 0 and d_ff % FF_CHUNK == 0
    return pl.pallas_call(
        functools.partial(_ffn_body, final_norm=final_norm),
        grid=(n // tm,),
        in_specs=[pl.BlockSpec((tm, d), lambda i: (i, 0))]
        + [_layer_spec(a.shape, layer) for a in (g, wg, wu, wd)] + [_const_spec(fg.shape)],
        out_specs=pl.BlockSpec((tm, d), lambda i: (i, 0)),
        out_shape=jax.ShapeDtypeStruct((n, d), jnp.float32),
        scratch_shapes=[pltpu.VMEM((tm, d_ff), jnp.bfloat16)],
        compiler_params=pltpu.CompilerParams(
            dimension_semantics=("arbitrary",), vmem_limit_bytes=VMEM_LIMIT_BYTES),
        name="ffn",
    )(x2d, g, wg, wu, wd, fg)


def _conv_body(x_ref, g_ref, w1_ref, b1_ref, wdw_ref, bdw_ref, lng_ref, lnb_ref, w2_ref, b2_ref,
               o_ref, u_ref, c_ref, *, row_chunk):
    ts, d = x_ref.shape[1], x_ref.shape[2]
    c = w2_ref.shape[0]
    n_slab = c // 128

    @pl.when(pl.program_id(1) == 0)
    def _():
        u_ref[:, 0:CONV_HALO, :] = jnp.zeros((n_slab, CONV_HALO, 128), jnp.float32)

    x = x_ref[0]
    xn = _rmsnorm_f32(x, g_ref[...]).astype(jnp.bfloat16)
    base = CONV_HALO - (CONV_WIDTH - 1)
    mxu_n = 256
    for jp in range(c // mxu_n):
        cs = slice(jp * mxu_n, (jp + 1) * mxu_n)
        gs = slice(c + jp * mxu_n, c + (jp + 1) * mxu_n)
        a = _mm(xn, w1_ref[:, cs]) + b1_ref[:, cs]
        gate = _mm(xn, w1_ref[:, gs]) + b1_ref[:, gs]
        glu = a * jax.nn.sigmoid(gate)
        for half in range(mxu_n // 128):
            j = jp * (mxu_n // 128) + half
            ls = slice(j * 128, (j + 1) * 128)
            u_ref[j, CONV_HALO:CONV_HALO + ts, :] = glu[:, half * 128:(half + 1) * 128]
            for r0 in range(0, ts, row_chunk):
                acc = jnp.broadcast_to(bdw_ref[:, ls], (row_chunk, 128))
                for k in range(CONV_WIDTH):
                    acc = acc + wdw_ref[k:k + 1, ls] * u_ref[j, r0 + base + k:r0 + base + k + row_chunk, :]
                c_ref[r0:r0 + row_chunk, ls] = acc
            u_ref[j, 0:CONV_HALO, :] = u_ref[j, ts:ts + CONV_HALO, :]

    v = c_ref[...]
    mu = jnp.mean(v, axis=-1, keepdims=True)
    vc = v - mu
    var = jnp.mean(vc * vc, axis=-1, keepdims=True)
    y = vc * lax.rsqrt(var + EPS) * lng_ref[...] + lnb_ref[...]
    y = (y * jax.nn.sigmoid(y)).astype(jnp.bfloat16)
    o_ref[0] = x + _mm(y, w2_ref[...]) + b2_ref[...]


def _conv_module(x, layer, mix_layer, g, w1, b1, wdw, bdw, lng, lnb, w2, b2):
    b, s, d = x.shape
    c = w2.shape[1]
    ts = TOKEN_TILE
    assert s % ts == 0 and c % 128 == 0
    return pl.pallas_call(
        functools.partial(_conv_body, row_chunk=64),
        grid=(b, s // ts),
        in_specs=[pl.BlockSpec((1, ts, d), lambda bi, si: (bi, si, 0)), _layer_spec(g.shape, mix_layer)]
        + [_layer_spec(a.shape, layer) for a in (w1, b1, wdw, bdw, lng, lnb, w2, b2)],
        out_specs=pl.BlockSpec((1, ts, d), lambda bi, si: (bi, si, 0)),
        out_shape=jax.ShapeDtypeStruct((b, s, d), jnp.float32),
        scratch_shapes=[pltpu.VMEM((c // 128, ts + CONV_HALO, 128), jnp.float32),
                        pltpu.VMEM((ts, c), jnp.float32)],
        compiler_params=pltpu.CompilerParams(
            dimension_semantics=("arbitrary", "arbitrary"), vmem_limit_bytes=VMEM_LIMIT_BYTES),
        name="conv_module",
    )(x, g, w1, b1, wdw, bdw, lng, lnb, w2, b2)


def _t5_bucket_np(dist):
    max_exact = N_BUCKETS // 2
    n = np.maximum(dist, 0)
    nf = np.maximum(n, 1).astype(np.float32)
    large = max_exact + (np.log(nf / np.float32(max_exact)) / np.float32(math.log(REL_MAX_DISTANCE / max_exact))
                         * np.float32(N_BUCKETS - max_exact)).astype(np.int32)
    large = np.minimum(large, N_BUCKETS - 1)
    return np.where(n < max_exact, n, large).astype(np.int32)


def _bucket_index_tables(n_back):
    i_idx = np.arange(n_back)[:, None]
    j_idx = np.arange(2 * n_back)[None, :]
    dist = i_idx + n_back - j_idx
    valid = (dist >= 0) & (dist <= n_back)
    tabs = [np.where(valid, _t5_bucket_np(dist * dil), -1) for _, dil in DILATION_PAIRS]
    return np.stack(tabs).astype(np.int32)


def _bias_body(tab_ref, idx_ref, o_ref):
    g = pl.program_id(0)
    idx = idx_ref[0]
    for h in range(HEADS_PER_GROUP):
        acc = jnp.full(idx.shape, NEG_INF, jnp.float32)
        for bkt in range(N_BUCKETS):
            acc = jnp.where(idx == bkt, tab_ref[bkt, g * HEADS_PER_GROUP + h], acc)
        o_ref[0, h] = acc


def _bias_tables(rel_bias, n_back):
    idx = jnp.asarray(_bucket_index_tables(n_back))
    return pl.pallas_call(
        _bias_body,
        grid=(N_GROUPS,),
        in_specs=[pl.BlockSpec(memory_space=pltpu.SMEM),
                  pl.BlockSpec((1, n_back, 2 * n_back), lambda g: (g, 0, 0))],
        out_specs=pl.BlockSpec((1, HEADS_PER_GROUP, n_back, 2 * n_back), lambda g: (g, 0, 0, 0)),
        out_shape=jax.ShapeDtypeStruct((N_GROUPS, HEADS_PER_GROUP, n_back, 2 * n_back), jnp.float32),
        name="rel_bias_tables",
    )(rel_bias, idx)


def _qkv_body(x_ref, g_ref, w_ref, *refs, d_attn):
    o_refs, scr_ref = refs[:N_GROUPS], refs[N_GROUPS]
    tm = x_ref.shape[0]
    xn = _rmsnorm_f32(x_ref[...], g_ref[...]).astype(jnp.bfloat16)
    for gi, (_, dil) in reversed(list(enumerate(DILATION_PAIRS))):
        for part in range(3):
            c0 = part * d_attn + gi * GROUP_WIDTH
            y = _mm(xn, w_ref[:, c0:c0 + GROUP_WIDTH])
            if part == 0:
                y = y * (HEAD_DIM ** -0.5)
            if dil == 1:
                o_refs[gi][:, part * GROUP_WIDTH:(part + 1) * GROUP_WIDTH] = y.astype(jnp.bfloat16)
                continue
            slot = (gi - 1) * 3 + part
            pad = 1 if dil % 8 == 0 else 0
            pitch = dil + pad
            for half in range(GROUP_WIDTH // 128):
                yh = y[:, half * 128:(half + 1) * 128]
                if pad == 0:
                    scr_ref[slot, half, 0:tm, :] = yh
                else:
                    for a in range(tm // dil):
                        scr_ref[slot, half, a * pitch:a * pitch + dil, :] = yh[a * dil:(a + 1) * dil]
            for r in range(dil):
                for half in range(GROUP_WIDTH // 128):
                    lane0 = (r * 3 + part) * GROUP_WIDTH + half * 128
                    o_refs[gi][:, lane0:lane0 + 128] = (
                        scr_ref[slot, half, pl.ds(r, tm // dil, stride=pitch), :].astype(jnp.bfloat16))


def _qkv_proj(x2d, layer, mix_layer, g, w):
    n, d = x2d.shape
    dq = w.shape[2]
    tm = 2 * TOKEN_TILE
    gw3 = 3 * GROUP_WIDTH
    assert DILATION_PAIRS[0][1] == 1 and n % tm == 0
    return pl.pallas_call(
        functools.partial(_qkv_body, d_attn=dq // 3),
        grid=(n // tm,),
        in_specs=[pl.BlockSpec((tm, d), lambda i: (i, 0)), _layer_spec(g.shape, mix_layer),
                  _layer_spec(w.shape, layer)],
        out_specs=[pl.BlockSpec((tm // dil, dil * gw3), lambda i: (i, 0)) for _, dil in DILATION_PAIRS],
        out_shape=[jax.ShapeDtypeStruct((n // dil, dil * gw3), jnp.bfloat16) for _, dil in DILATION_PAIRS],
        scratch_shapes=[pltpu.VMEM((3 * (N_GROUPS - 1), GROUP_WIDTH // 128, tm + tm // 8, 128), jnp.float32)],
        compiler_params=pltpu.CompilerParams(
            dimension_semantics=("arbitrary",), vmem_limit_bytes=VMEM_LIMIT_BYTES),
        name="qkv_proj",
    )(x2d, g, w)


def _attn_body(cur_ref, prev_ref, bias_ref, o_ref, l_ref, *, n_back):
    first = pl.program_id(2) == 0
    tq = cur_ref.shape[1]
    pair_w = 2 * HEAD_DIM
    lane = lax.broadcasted_iota(jnp.int32, (n_back, pair_w), 1)
    low = lane < HEAD_DIM
    kv_low = lax.broadcasted_iota(jnp.int32, (2 * n_back, pair_w), 1) < HEAD_DIM
    pairs = HEADS_PER_GROUP // 2
    for rr, hp in [(rr, hp) for rr in range(cur_ref.shape[2] // (3 * GROUP_WIDTH)) for hp in range(pairs)]:
        qs, ks, vs = (slice((rr * 3 + part) * GROUP_WIDTH + hp * pair_w,
                            (rr * 3 + part) * GROUP_WIDTH + (hp + 1) * pair_w) for part in range(3))
        ps = slice(rr * GROUP_WIDTH + hp * pair_w, rr * GROUP_WIDTH + (hp + 1) * pair_w)
        bias = bias_ref[0, 2 * hp:2 * hp + 2].reshape(2 * n_back, 2 * n_back)
        for i in range(tq // n_back):
            rows = slice(i * n_back, (i + 1) * n_back)
            q = cur_ref[0, rows, qs]
            if i == 0:
                k = jnp.concatenate([prev_ref[0, :, ks], cur_ref[0, 0:n_back, ks]], axis=0)
                v = jnp.concatenate([prev_ref[0, :, vs], cur_ref[0, 0:n_back, vs]], axis=0)
            else:
                k = cur_ref[0, (i - 1) * n_back:(i + 1) * n_back, ks]
                v = cur_ref[0, (i - 1) * n_back:(i + 1) * n_back, vs]
            zero = jnp.zeros_like(q)
            q2 = jnp.concatenate([jnp.where(low, q, zero), jnp.where(low, zero, q)], axis=0)
            logits = lax.dot_general(q2, k, (((1,), (1,)), ((), ())),
                                     preferred_element_type=jnp.float32) + bias
            if i == 0:
                col = lax.broadcasted_iota(jnp.int32, logits.shape, 1)
                logits = jnp.where(jnp.logical_and(first, col < n_back), NEG_INF, logits)
            m = jnp.max(logits, axis=-1, keepdims=True)
            p = jnp.exp(logits - m)
            den = jnp.sum(p, axis=-1, keepdims=True)
            pb = p.astype(jnp.bfloat16)
            zv = jnp.zeros_like(v)
            v2 = jnp.concatenate([jnp.where(kv_low, v, zv), jnp.where(kv_low, zv, v)], axis=0)
            p2 = jnp.concatenate([pb[:n_back], pb[n_back:]], axis=1)
            o = _mm(p2, v2)
            lse = m + jnp.log(den)
            o_ref[0, rows, ps] = o / jnp.where(low, den[:n_back], den[n_back:])
            l_ref[0, rows, ps] = jnp.where(low, lse[:n_back], lse[n_back:])


def _attn_group(view, bias, g, dilation, n_back):
    b, sub, _ = view.shape
    max_rows = ATTN_Q_BLOCKS * n_back
    tq = min(max_rows, sub)
    res = max(1, min(dilation, max_rows // tq))
    assert sub % tq == 0 and dilation % res == 0
    nprev = tq // n_back
    qkv_w = res * 3 * GROUP_WIDTH
    out_spec = pl.BlockSpec((1, tq, res * GROUP_WIDTH), lambda bi, r, n: (bi, n, r))
    out_sds = jax.ShapeDtypeStruct((b, sub, dilation * GROUP_WIDTH), jnp.float32)
    return pl.pallas_call(
        functools.partial(_attn_body, n_back=n_back),
        grid=(b, dilation // res, sub // tq),
        in_specs=[pl.BlockSpec((1, tq, qkv_w), lambda bi, r, n: (bi, n, r)),
                  pl.BlockSpec((1, n_back, qkv_w), lambda bi, r, n: (bi, jnp.maximum(n * nprev - 1, 0), r)),
                  pl.BlockSpec((1, HEADS_PER_GROUP, n_back, 2 * n_back), lambda bi, r, n: (g, 0, 0, 0))],
        out_specs=[out_spec, out_spec],
        out_shape=[out_sds, out_sds],
        compiler_params=pltpu.CompilerParams(
            dimension_semantics=("arbitrary", "arbitrary", "arbitrary"), vmem_limit_bytes=VMEM_LIMIT_BYTES),
        name=f"dilated_attn_g{g}",
    )(view, view, bias)


def _attn_out_ffn_body(x_ref, o0_ref, o1_ref, o2_ref, l0_ref, l1_ref, l2_ref, wo_ref,
                       g_ref, wg_ref, wu_ref, wd_ref, fg_ref, y_ref, scr_ref, h_ref, *, final_norm):
    tm = x_ref.shape[0]
    halves = GROUP_WIDTH // 128

    def token_major(ref, gi, slot):
        dil = DILATION_PAIRS[gi][1]
        if dil == 1:
            return ref[...]
        for r in range(dil):
            for half in range(halves):
                lane0 = r * GROUP_WIDTH + half * 128
                scr_ref[slot, half, pl.ds(r, tm // dil, stride=dil), :] = ref[:, lane0:lane0 + 128]
        return jnp.concatenate([scr_ref[slot, half] for half in range(halves)], axis=1)

    o_refs, l_refs = (o0_ref, o1_ref, o2_ref), (l0_ref, l1_ref, l2_ref)
    ls = [token_major(l_refs[gi], gi, gi) for gi in range(N_GROUPS)]
    mx = jnp.maximum(jnp.maximum(ls[0], ls[1]), ls[2])
    es = [jnp.exp(l - mx) for l in ls]
    tot = es[0] + es[1] + es[2]
    x = x_ref[...]
    for gi in range(N_GROUPS):
        og = (token_major(o_refs[gi], gi, N_GROUPS + gi) * (es[gi] / tot)).astype(jnp.bfloat16)
        x = x + _mm(og, wo_ref[gi * GROUP_WIDTH:(gi + 1) * GROUP_WIDTH, :])
    y_ref[...] = _swiglu_residual(x, g_ref, wg_ref, wu_ref, wd_ref, fg_ref, h_ref, final_norm)


def _attn_out_ffn(x2d, outs, lses, attn_layer, w_o, layer, g, wg, wu, wd, fg, final_norm):
    n, d = x2d.shape
    d_ff = wg.shape[2]
    tm = TOKEN_TILE
    tok = lambda w: pl.BlockSpec((tm, w), lambda i: (i, 0))
    grp = [pl.BlockSpec((tm // dil, dil * GROUP_WIDTH), lambda i: (i, 0)) for _, dil in DILATION_PAIRS]
    return pl.pallas_call(
        functools.partial(_attn_out_ffn_body, final_norm=final_norm),
        grid=(n // tm,),
        in_specs=[tok(d)] + grp + grp + [_layer_spec(w_o.shape, attn_layer)]
        + [_layer_spec(a.shape, layer) for a in (g, wg, wu, wd)] + [_const_spec(fg.shape)],
        out_specs=tok(d),
        out_shape=jax.ShapeDtypeStruct((n, d), jnp.float32),
        scratch_shapes=[pltpu.VMEM((2 * N_GROUPS, GROUP_WIDTH // 128, tm, 128), jnp.float32),
                        pltpu.VMEM((tm, d_ff), jnp.bfloat16)],
        compiler_params=pltpu.CompilerParams(
            dimension_semantics=("arbitrary",), vmem_limit_bytes=VMEM_LIMIT_BYTES),
        name="attn_out_ffn",
    )(x2d, *outs, *lses, w_o, g, wg, wu, wd, fg)


def _dilated_attention(x2d, b, layer, mix_layer, g, w_qkv, bias):
    n = x2d.shape[0]
    views = _qkv_proj(x2d, layer, mix_layer, g, w_qkv)
    outs, lses = [], []
    for gi, (window, dil) in enumerate(DILATION_PAIRS):
        o, l = _attn_group(views[gi].reshape(b, n // b // dil, -1), bias, gi, dil, window // dil)
        outs.append(o.reshape(n // dil, dil * GROUP_WIDTH))
        lses.append(l.reshape(n // dil, dil * GROUP_WIDTH))
    return outs, lses


def kernel(x, norm_mix, norm_ffn, final_norm, conv_w_pw1, conv_b_pw1, conv_w_dw, conv_b_dw, conv_ln_g, conv_ln_b, conv_w_pw2, conv_b_pw2, attn_w_qkv, attn_w_o, rel_bias, ffn_w_gate, ffn_w_up, ffn_w_down):
    b, s, d = x.shape
    depth = norm_mix.shape[0]
    n_back = DILATION_PAIRS[0][0] // DILATION_PAIRS[0][1]
    assert all(w // dl == n_back and s % w == 0 for w, dl in DILATION_PAIRS)
    bias = _bias_tables(rel_bias, n_back)
    w_pw1, w_pw2, w_qkv, w_o, w_gate, w_up, w_down = (
        _to_bf16(w) for w in (conv_w_pw1, conv_w_pw2, attn_w_qkv, attn_w_o, ffn_w_gate, ffn_w_up, ffn_w_down))
    rows = lambda v: v.reshape(v.shape[0], 1, v.shape[1])
    g_mix, g_ffn = rows(norm_mix), rows(norm_ffn)
    b_pw1, b_dw, ln_g, ln_b, b_pw2 = (rows(v) for v in (conv_b_pw1, conv_b_dw, conv_ln_g, conv_ln_b, conv_b_pw2))
    fg = final_norm.reshape(1, d)
    for i in range(depth):
        j = i // 2
        last = i == depth - 1
        if i % 2 == 0:
            x = _conv_module(x, j, i, g_mix, w_pw1, b_pw1, conv_w_dw, b_dw, ln_g, ln_b, w_pw2, b_pw2)
            x = _ffn(x.reshape(b * s, d), i, g_ffn, w_gate, w_up, w_down, fg, last).reshape(b, s, d)
        else:
            x2d = x.reshape(b * s, d)
            outs, lses = _dilated_attention(x2d, b, j, i, g_mix, w_qkv, bias)
            x = _attn_out_ffn(x2d, outs, lses, j, w_o, i, g_ffn, w_gate, w_up, w_down, fg, last).reshape(b, s, d)
    return x
```

```python
import functools
import math
from typing import NamedTuple

import numpy as np
import jax
import jax.numpy as jnp
from jax import lax
from jax.experimental import pallas as pl
from jax.experimental.pallas import tpu as pltpu

EPS = 1e-6
NEG_INF = -1e30
HEAD_DIM = 64
HEADS_PER_GROUP = 4
GROUP_WIDTH = HEADS_PER_GROUP * HEAD_DIM
DILATION_PAIRS = ((128, 1), (512, 4), (2048, 16))
N_GROUPS = len(DILATION_PAIRS)
N_BUCKETS = 32
REL_MAX_DISTANCE = 2048
CONV_WIDTH = 31
CONV_HALO = 32

VMEM_LIMIT_BYTES = 56 * 1024 * 1024
TOKEN_TILE = 512
ATTN_Q_BLOCKS = 16


def _const_spec(shape):
    nd = len(shape)
    return pl.BlockSpec(shape, lambda *_: (0,) * nd, pipeline_mode=pl.Buffered(1))


def _layer_spec(stack_shape, layer):
    return pl.BlockSpec((None,) + tuple(stack_shape[1:]), lambda *_: (layer, 0, 0), pipeline_mode=pl.Buffered(1))


def _cast_body(x_ref, o_ref):
    o_ref[...] = x_ref[...].astype(o_ref.dtype)


def _to_bf16(w):
    l, r, c = w.shape
    rows = 512
    assert (l * r) % rows == 0
    out = pl.pallas_call(
        _cast_body,
        grid=(l * r // rows,),
        in_specs=[pl.BlockSpec((rows, c), lambda i: (i, 0))],
        out_specs=pl.BlockSpec((rows, c), lambda i: (i, 0)),
        out_shape=jax.ShapeDtypeStruct((l * r, c), jnp.bfloat16),
        compiler_params=pltpu.CompilerParams(
            dimension_semantics=("arbitrary",), vmem_limit_bytes=VMEM_LIMIT_BYTES),
        name="to_bf16",
    )(w.reshape(l * r, c))
    return out.reshape(l, r, c)


def _rmsnorm_f32(x, g):
    return x * lax.rsqrt(jnp.mean(x * x, axis=-1, keepdims=True) + EPS) * g


def _mm(a, b):
    return jnp.dot(a, b, preferred_element_type=jnp.float32)


FF_CHUNK = 256


class _FfnWeights(NamedTuple):
    layer: int
    hbm: tuple
    vmem: tuple
    stage: tuple
    sem: object

    def chunk_copies(self, c, slot):
        cs = pl.ds(c * FF_CHUNK, FF_CHUNK)
        srcs = (self.hbm[0].at[self.layer, :, cs], self.hbm[1].at[self.layer, :, cs],
                self.hbm[2].at[self.layer, cs, :])
        return [pltpu.make_async_copy(src, stg.at[slot], self.sem.at[slot, i])
                for i, (src, stg) in enumerate(zip(srcs, self.stage))]

    @staticmethod
    def scratch_shapes(d, d_ff):
        bf, f32 = jnp.bfloat16, jnp.float32
        return [pltpu.VMEM((d, d_ff), bf), pltpu.VMEM((d, d_ff), bf), pltpu.VMEM((d_ff, d), bf),
                pltpu.VMEM((2, d, FF_CHUNK), f32), pltpu.VMEM((2, d, FF_CHUNK), f32),
                pltpu.VMEM((2, FF_CHUNK, d), f32), pltpu.SemaphoreType.DMA((2, 3))]

    @classmethod
    def from_refs(cls, layer, hbm_refs, scratch_refs):
        return cls(layer, tuple(hbm_refs), tuple(scratch_refs[0:3]), tuple(scratch_refs[3:6]), scratch_refs[6])


def _swiglu_residual(x, g_ref, fw, fg_ref, h_ref, final_norm, stream):
    wg_ref, wu_ref, wd_ref = fw.vmem
    n_chunks = wg_ref.shape[1] // FF_CHUNK
    if stream:
        for cp in fw.chunk_copies(0, 0):
            cp.start()
    xn = _rmsnorm_f32(x, g_ref[...]).astype(jnp.bfloat16)
    for c in range(n_chunks):
        cs = slice(c * FF_CHUNK, (c + 1) * FF_CHUNK)
        if stream:
            slot = c % 2
            if c + 1 < n_chunks:
                for cp in fw.chunk_copies(c + 1, 1 - slot):
                    cp.start()
            for cp in fw.chunk_copies(c, slot):
                cp.wait()
            wg_ref[:, cs] = fw.stage[0][slot].astype(jnp.bfloat16)
            wu_ref[:, cs] = fw.stage[1][slot].astype(jnp.bfloat16)
            wd_ref[cs, :] = fw.stage[2][slot].astype(jnp.bfloat16)
        gate = _mm(xn, wg_ref[:, cs])
        up = _mm(xn, wu_ref[:, cs])
        h_ref[:, cs] = (gate * jax.nn.sigmoid(gate) * up).astype(jnp.bfloat16)
    y = x + _mm(h_ref[...], wd_ref[...])
    if final_norm:
        y = _rmsnorm_f32(y, fg_ref[...])
    return y


def _swiglu_residual_steps(x_fn, g_ref, fw, fg_ref, h_ref, o_ref, final_norm):
    first = pl.program_id(0) == 0

    @pl.when(first)
    def _():
        o_ref[...] = _swiglu_residual(x_fn(), g_ref, fw, fg_ref, h_ref, final_norm, stream=True)

    @pl.when(jnp.logical_not(first))
    def _():
        o_ref[...] = _swiglu_residual(x_fn(), g_ref, fw, fg_ref, h_ref, final_norm, stream=False)


def _ffn_body(x_ref, g_ref, wg_hbm, wu_hbm, wd_hbm, fg_ref, o_ref, h_ref, *w_scratch, layer, final_norm):
    fw = _FfnWeights.from_refs(layer, (wg_hbm, wu_hbm, wd_hbm), w_scratch)
    _swiglu_residual_steps(lambda: x_ref[...], g_ref, fw, fg_ref, h_ref, o_ref, final_norm)


def _ffn(x2d, layer, g, wg, wu, wd, fg, final_norm):
    n, d = x2d.shape
    d_ff = wg.shape[2]
    tm = 2 * TOKEN_TILE
    assert n % tm == 0 and d_ff % FF_CHUNK == 0
    hbm = pl.BlockSpec(memory_space=pl.ANY)
    return pl.pallas_call(
        functools.partial(_ffn_body, layer=layer, final_norm=final_norm),
        grid=(n // tm,),
        in_specs=[pl.BlockSpec((tm, d), lambda i: (i, 0)), _layer_spec(g.shape, layer), hbm, hbm, hbm,
                  _const_spec(fg.shape)],
        out_specs=pl.BlockSpec((tm, d), lambda i: (i, 0)),
        out_shape=jax.ShapeDtypeStruct((n, d), jnp.float32),
        scratch_shapes=[pltpu.VMEM((tm, d_ff), jnp.bfloat16)] + _FfnWeights.scratch_shapes(d, d_ff),
        compiler_params=pltpu.CompilerParams(
            dimension_semantics=("arbitrary",), vmem_limit_bytes=VMEM_LIMIT_BYTES),
        name="ffn",
    )(x2d, g, wg, wu, wd, fg)


def _conv_body(x_ref, g_ref, w1_ref, b1_ref, wdw_ref, bdw_ref, lng_ref, lnb_ref, w2_ref, b2_ref,
               o_ref, u_ref, c_ref, *, row_chunk):
    ts, d = x_ref.shape[1], x_ref.shape[2]
    c = w2_ref.shape[0]
    n_slab = c // 128

    @pl.when(pl.program_id(1) == 0)
    def _():
        u_ref[:, 0:CONV_HALO, :] = jnp.zeros((n_slab, CONV_HALO, 128), jnp.float32)

    x = x_ref[0]
    xn = _rmsnorm_f32(x, g_ref[...]).astype(jnp.bfloat16)
    base = CONV_HALO - (CONV_WIDTH - 1)
    mxu_n = 256
    for jp in range(c // mxu_n):
        cs = slice(jp * mxu_n, (jp + 1) * mxu_n)
        gs = slice(c + jp * mxu_n, c + (jp + 1) * mxu_n)
        a = _mm(xn, w1_ref[:, cs]) + b1_ref[:, cs]
        gate = _mm(xn, w1_ref[:, gs]) + b1_ref[:, gs]
        glu = a * jax.nn.sigmoid(gate)
        for half in range(mxu_n // 128):
            j = jp * (mxu_n // 128) + half
            ls = slice(j * 128, (j + 1) * 128)
            u_ref[j, CONV_HALO:CONV_HALO + ts, :] = glu[:, half * 128:(half + 1) * 128]
            for r0 in range(0, ts, row_chunk):
                acc = jnp.broadcast_to(bdw_ref[:, ls], (row_chunk, 128))
                for k in range(CONV_WIDTH):
                    acc = acc + wdw_ref[k:k + 1, ls] * u_ref[j, r0 + base + k:r0 + base + k + row_chunk, :]
                c_ref[r0:r0 + row_chunk, ls] = acc
            u_ref[j, 0:CONV_HALO, :] = u_ref[j, ts:ts + CONV_HALO, :]

    v = c_ref[...]
    mu = jnp.mean(v, axis=-1, keepdims=True)
    vc = v - mu
    var = jnp.mean(vc * vc, axis=-1, keepdims=True)
    y = vc * lax.rsqrt(var + EPS) * lng_ref[...] + lnb_ref[...]
    y = (y * jax.nn.sigmoid(y)).astype(jnp.bfloat16)
    o_ref[0] = x + _mm(y, w2_ref[...]) + b2_ref[...]


def _conv_module(x, layer, mix_layer, g, w1, b1, wdw, bdw, lng, lnb, w2, b2):
    b, s, d = x.shape
    c = w2.shape[1]
    ts = TOKEN_TILE
    assert s % ts == 0 and c % 128 == 0
    return pl.pallas_call(
        functools.partial(_conv_body, row_chunk=64),
        grid=(b, s // ts),
        in_specs=[pl.BlockSpec((1, ts, d), lambda bi, si: (bi, si, 0)), _layer_spec(g.shape, mix_layer)]
        + [_layer_spec(a.shape, layer) for a in (w1, b1, wdw, bdw, lng, lnb, w2, b2)],
        out_specs=pl.BlockSpec((1, ts, d), lambda bi, si: (bi, si, 0)),
        out_shape=jax.ShapeDtypeStruct((b, s, d), jnp.float32),
        scratch_shapes=[pltpu.VMEM((c // 128, ts + CONV_HALO, 128), jnp.float32),
                        pltpu.VMEM((ts, c), jnp.float32)],
        compiler_params=pltpu.CompilerParams(
            dimension_semantics=("arbitrary", "arbitrary"), vmem_limit_bytes=VMEM_LIMIT_BYTES),
        name="conv_module",
    )(x, g, w1, b1, wdw, bdw, lng, lnb, w2, b2)


def _t5_bucket_np(dist):
    max_exact = N_BUCKETS // 2
    n = np.maximum(dist, 0)
    nf = np.maximum(n, 1).astype(np.float32)
    large = max_exact + (np.log(nf / np.float32(max_exact)) / np.float32(math.log(REL_MAX_DISTANCE / max_exact))
                         * np.float32(N_BUCKETS - max_exact)).astype(np.int32)
    large = np.minimum(large, N_BUCKETS - 1)
    return np.where(n < max_exact, n, large).astype(np.int32)


def _bucket_index_tables(n_back):
    i_idx = np.arange(n_back)[:, None]
    j_idx = np.arange(2 * n_back)[None, :]
    dist = i_idx + n_back - j_idx
    valid = (dist >= 0) & (dist <= n_back)
    tabs = [np.where(valid, _t5_bucket_np(dist * dil), -1) for _, dil in DILATION_PAIRS]
    return np.stack(tabs).astype(np.int32)


def _bias_body(tab_ref, idx_ref, o_ref):
    g = pl.program_id(0)
    idx = idx_ref[0]
    for h in range(HEADS_PER_GROUP):
        acc = jnp.full(idx.shape, NEG_INF, jnp.float32)
        for bkt in range(N_BUCKETS):
            acc = jnp.where(idx == bkt, tab_ref[bkt, g * HEADS_PER_GROUP + h], acc)
        o_ref[0, h] = acc


def _bias_tables(rel_bias, n_back):
    idx = jnp.asarray(_bucket_index_tables(n_back))
    return pl.pallas_call(
        _bias_body,
        grid=(N_GROUPS,),
        in_specs=[pl.BlockSpec(memory_space=pltpu.SMEM),
                  pl.BlockSpec((1, n_back, 2 * n_back), lambda g: (g, 0, 0))],
        out_specs=pl.BlockSpec((1, HEADS_PER_GROUP, n_back, 2 * n_back), lambda g: (g, 0, 0, 0)),
        out_shape=jax.ShapeDtypeStruct((N_GROUPS, HEADS_PER_GROUP, n_back, 2 * n_back), jnp.float32),
        name="rel_bias_tables",
    )(rel_bias, idx)


def _qkv_body(x_ref, g_ref, w_ref, *refs, d_attn):
    o_refs, scr_ref = refs[:N_GROUPS], refs[N_GROUPS]
    tm = x_ref.shape[0]
    xn = _rmsnorm_f32(x_ref[...], g_ref[...]).astype(jnp.bfloat16)
    for gi, (_, dil) in reversed(list(enumerate(DILATION_PAIRS))):
        for part in range(3):
            c0 = part * d_attn + gi * GROUP_WIDTH
            y = _mm(xn, w_ref[:, c0:c0 + GROUP_WIDTH])
            if part == 0:
                y = y * (HEAD_DIM ** -0.5)
            if dil == 1:
                o_refs[gi][:, part * GROUP_WIDTH:(part + 1) * GROUP_WIDTH] = y.astype(jnp.bfloat16)
                continue
            slot = (gi - 1) * 3 + part
            pad = 1 if dil % 8 == 0 else 0
            pitch = dil + pad
            for half in range(GROUP_WIDTH // 128):
                yh = y[:, half * 128:(half + 1) * 128]
                if pad == 0:
                    scr_ref[slot, half, 0:tm, :] = yh
                else:
                    for a in range(tm // dil):
                        scr_ref[slot, half, a * pitch:a * pitch + dil, :] = yh[a * dil:(a + 1) * dil]
            for r in range(dil):
                for half in range(GROUP_WIDTH // 128):
                    lane0 = (r * 3 + part) * GROUP_WIDTH + half * 128
                    o_refs[gi][:, lane0:lane0 + 128] = (
                        scr_ref[slot, half, pl.ds(r, tm // dil, stride=pitch), :].astype(jnp.bfloat16))


def _qkv_proj(x2d, layer, mix_layer, g, w):
    n, d = x2d.shape
    dq = w.shape[2]
    tm = 2 * TOKEN_TILE
    gw3 = 3 * GROUP_WIDTH
    assert DILATION_PAIRS[0][1] == 1 and n % tm == 0
    return pl.pallas_call(
        functools.partial(_qkv_body, d_attn=dq // 3),
        grid=(n // tm,),
        in_specs=[pl.BlockSpec((tm, d), lambda i: (i, 0)), _layer_spec(g.shape, mix_layer),
                  _layer_spec(w.shape, layer)],
        out_specs=[pl.BlockSpec((tm // dil, dil * gw3), lambda i: (i, 0)) for _, dil in DILATION_PAIRS],
        out_shape=[jax.ShapeDtypeStruct((n // dil, dil * gw3), jnp.bfloat16) for _, dil in DILATION_PAIRS],
        scratch_shapes=[pltpu.VMEM((3 * (N_GROUPS - 1), GROUP_WIDTH // 128, tm + tm // 8, 128), jnp.float32)],
        compiler_params=pltpu.CompilerParams(
            dimension_semantics=("arbitrary",), vmem_limit_bytes=VMEM_LIMIT_BYTES),
        name="qkv_proj",
    )(x2d, g, w)


def _attn_body(cur_ref, prev_ref, bias_ref, o_ref, l_ref, *, n_back):
    first = pl.program_id(2) == 0
    tq = cur_ref.shape[1]
    pair_w = 2 * HEAD_DIM
    lane = lax.broadcasted_iota(jnp.int32, (n_back, pair_w), 1)
    low = lane < HEAD_DIM
    kv_low = lax.broadcasted_iota(jnp.int32, (2 * n_back, pair_w), 1) < HEAD_DIM
    pairs = HEADS_PER_GROUP // 2
    for rr, hp in [(rr, hp) for rr in range(cur_ref.shape[2] // (3 * GROUP_WIDTH)) for hp in range(pairs)]:
        qs, ks, vs = (slice((rr * 3 + part) * GROUP_WIDTH + hp * pair_w,
                            (rr * 3 + part) * GROUP_WIDTH + (hp + 1) * pair_w) for part in range(3))
        ps = slice(rr * GROUP_WIDTH + hp * pair_w, rr * GROUP_WIDTH + (hp + 1) * pair_w)
        bias = bias_ref[0, 2 * hp:2 * hp + 2].reshape(2 * n_back, 2 * n_back)
        for i in range(tq // n_back):
            rows = slice(i * n_back, (i + 1) * n_back)
            q = cur_ref[0, rows, qs]
            if i == 0:
                k = jnp.concatenate([prev_ref[0, :, ks], cur_ref[0, 0:n_back, ks]], axis=0)
                v = jnp.concatenate([prev_ref[0, :, vs], cur_ref[0, 0:n_back, vs]], axis=0)
            else:
                k = cur_ref[0, (i - 1) * n_back:(i + 1) * n_back, ks]
                v = cur_ref[0, (i - 1) * n_back:(i + 1) * n_back, vs]
            zero = jnp.zeros_like(q)
            q2 = jnp.concatenate([jnp.where(low, q, zero), jnp.where(low, zero, q)], axis=0)
            logits = lax.dot_general(q2, k, (((1,), (1,)), ((), ())),
                                     preferred_element_type=jnp.float32) + bias
            if i == 0:
                col = lax.broadcasted_iota(jnp.int32, logits.shape, 1)
                logits = jnp.where(jnp.logical_and(first, col < n_back), NEG_INF, logits)
            m = jnp.max(logits, axis=-1, keepdims=True)
            p = jnp.exp(logits - m)
            den = jnp.sum(p, axis=-1, keepdims=True)
            pb = p.astype(jnp.bfloat16)
            zv = jnp.zeros_like(v)
            v2 = jnp.concatenate([jnp.where(kv_low, v, zv), jnp.where(kv_low, zv, v)], axis=0)
            p2 = jnp.concatenate([pb[:n_back], pb[n_back:]], axis=1)
            o = _mm(p2, v2)
            lse = m + jnp.log(den)
            o_ref[0, rows, ps] = o / jnp.where(low, den[:n_back], den[n_back:])
            l_ref[0, rows, ps] = jnp.where(low, lse[:n_back], lse[n_back:])


def _attn_group(view, bias, g, dilation, n_back):
    b, sub, _ = view.shape
    max_rows = ATTN_Q_BLOCKS * n_back
    tq = min(max_rows, sub)
    res = max(1, min(dilation, max_rows // tq))
    assert sub % tq == 0 and dilation % res == 0
    nprev = tq // n_back
    qkv_w = res * 3 * GROUP_WIDTH
    out_spec = pl.BlockSpec((1, tq, res * GROUP_WIDTH), lambda bi, r, n: (bi, n, r))
    out_sds = jax.ShapeDtypeStruct((b, sub, dilation * GROUP_WIDTH), jnp.float32)
    return pl.pallas_call(
        functools.partial(_attn_body, n_back=n_back),
        grid=(b, dilation // res, sub // tq),
        in_specs=[pl.BlockSpec((1, tq, qkv_w), lambda bi, r, n: (bi, n, r)),
                  pl.BlockSpec((1, n_back, qkv_w), lambda bi, r, n: (bi, jnp.maximum(n * nprev - 1, 0), r)),
                  pl.BlockSpec((1, HEADS_PER_GROUP, n_back, 2 * n_back), lambda bi, r, n: (g, 0, 0, 0))],
        out_specs=[out_spec, out_spec],
        out_shape=[out_sds, out_sds],
        compiler_params=pltpu.CompilerParams(
            dimension_semantics=("arbitrary", "arbitrary", "arbitrary"), vmem_limit_bytes=VMEM_LIMIT_BYTES),
        name=f"dilated_attn_g{g}",
    )(view, view, bias)


def _attn_out_ffn_body(x_ref, o0_ref, o1_ref, o2_ref, l0_ref, l1_ref, l2_ref, wo_ref,
                       g_ref, wg_hbm, wu_hbm, wd_hbm, fg_ref, y_ref, scr_ref, x1_ref, h_ref, *w_scratch,
                       layer, final_norm):
    tm = x_ref.shape[0]
    halves = GROUP_WIDTH // 128

    def token_major(ref, gi, slot):
        dil = DILATION_PAIRS[gi][1]
        if dil == 1:
            return ref[...]
        for r in range(dil):
            for half in range(halves):
                lane0 = r * GROUP_WIDTH + half * 128
                scr_ref[slot, half, pl.ds(r, tm // dil, stride=dil), :] = ref[:, lane0:lane0 + 128]
        return jnp.concatenate([scr_ref[slot, half] for half in range(halves)], axis=1)

    o_refs, l_refs = (o0_ref, o1_ref, o2_ref), (l0_ref, l1_ref, l2_ref)
    ls = [token_major(l_refs[gi], gi, gi) for gi in range(N_GROUPS)]
    mx = jnp.maximum(jnp.maximum(ls[0], ls[1]), ls[2])
    es = [jnp.exp(l - mx) for l in ls]
    tot = es[0] + es[1] + es[2]
    x = x_ref[...]
    for gi in range(N_GROUPS):
        og = (token_major(o_refs[gi], gi, N_GROUPS + gi) * (es[gi] / tot)).astype(jnp.bfloat16)
        x = x + _mm(og, wo_ref[gi * GROUP_WIDTH:(gi + 1) * GROUP_WIDTH, :])
    x1_ref[...] = x
    fw = _FfnWeights.from_refs(layer, (wg_hbm, wu_hbm, wd_hbm), w_scratch)
    _swiglu_residual_steps(lambda: x1_ref[...], g_ref, fw, fg_ref, h_ref, y_ref, final_norm)


def _attn_out_ffn(x2d, outs, lses, attn_layer, w_o, layer, g, wg, wu, wd, fg, final_norm):
    n, d = x2d.shape
    d_ff = wg.shape[2]
    tm = TOKEN_TILE
    tok = lambda w: pl.BlockSpec((tm, w), lambda i: (i, 0))
    grp = [pl.BlockSpec((tm // dil, dil * GROUP_WIDTH), lambda i: (i, 0)) for _, dil in DILATION_PAIRS]
    hbm = pl.BlockSpec(memory_space=pl.ANY)
    return pl.pallas_call(
        functools.partial(_attn_out_ffn_body, layer=layer, final_norm=final_norm),
        grid=(n // tm,),
        in_specs=[tok(d)] + grp + grp + [_layer_spec(w_o.shape, attn_layer), _layer_spec(g.shape, layer),
                                         hbm, hbm, hbm, _const_spec(fg.shape)],
        out_specs=tok(d),
        out_shape=jax.ShapeDtypeStruct((n, d), jnp.float32),
        scratch_shapes=[pltpu.VMEM((2 * N_GROUPS, GROUP_WIDTH // 128, tm, 128), jnp.float32),
                        pltpu.VMEM((tm, d), jnp.float32),
                        pltpu.VMEM((tm, d_ff), jnp.bfloat16)] + _FfnWeights.scratch_shapes(d, d_ff),
        compiler_params=pltpu.CompilerParams(
            dimension_semantics=("arbitrary",), vmem_limit_bytes=VMEM_LIMIT_BYTES),
        name="attn_out_ffn",
    )(x2d, *outs, *lses, w_o, g, wg, wu, wd, fg)


def _dilated_attention(x2d, b, layer, mix_layer, g, w_qkv, bias):
    n = x2d.shape[0]
    views = _qkv_proj(x2d, layer, mix_layer, g, w_qkv)
    outs, lses = [], []
    for gi, (window, dil) in enumerate(DILATION_PAIRS):
        o, l = _attn_group(views[gi].reshape(b, n // b // dil, -1), bias, gi, dil, window // dil)
        outs.append(o.reshape(n // dil, dil * GROUP_WIDTH))
        lses.append(l.reshape(n // dil, dil * GROUP_WIDTH))
    return outs, lses


def kernel(x, norm_mix, norm_ffn, final_norm, conv_w_pw1, conv_b_pw1, conv_w_dw, conv_b_dw, conv_ln_g, conv_ln_b, conv_w_pw2, conv_b_pw2, attn_w_qkv, attn_w_o, rel_bias, ffn_w_gate, ffn_w_up, ffn_w_down):
    b, s, d = x.shape
    depth = norm_mix.shape[0]
    n_back = DILATION_PAIRS[0][0] // DILATION_PAIRS[0][1]
    assert all(w // dl == n_back and s % w == 0 for w, dl in DILATION_PAIRS)
    bias = _bias_tables(rel_bias, n_back)
    w_pw1, w_pw2, w_qkv, w_o = (_to_bf16(w) for w in (conv_w_pw1, conv_w_pw2, attn_w_qkv, attn_w_o))
    w_gate, w_up, w_down = ffn_w_gate, ffn_w_up, ffn_w_down
    rows = lambda v: v.reshape(v.shape[0], 1, v.shape[1])
    g_mix, g_ffn = rows(norm_mix), rows(norm_ffn)
    b_pw1, b_dw, ln_g, ln_b, b_pw2 = (rows(v) for v in (conv_b_pw1, conv_b_dw, conv_ln_g, conv_ln_b, conv_b_pw2))
    fg = final_norm.reshape(1, d)
    for i in range(depth):
        j = i // 2
        last = i == depth - 1
        if i % 2 == 0:
            x = _conv_module(x, j, i, g_mix, w_pw1, b_pw1, conv_w_dw, b_dw, ln_g, ln_b, w_pw2, b_pw2)
            x = _ffn(x.reshape(b * s, d), i, g_ffn, w_gate, w_up, w_down, fg, last).reshape(b, s, d)
        else:
            x2d = x.reshape(b * s, d)
            outs, lses = _dilated_attention(x2d, b, j, i, g_mix, w_qkv, bias)
            x = _attn_out_ffn(x2d, outs, lses, j, w_o, i, g_ffn, w_gate, w_up, w_down, fg, last).reshape(b, s, d)
    return x
```
